```python
import math
import jax, jax.numpy as jnp
from jax import lax
import numpy as np

D_MODEL = 1024
BATCH = 2
SEQ = 8192
DEPTH = 2

MEM_TOKENS = 256
ROPE_THETA = 10000.0
RMS_EPS = 1e-6
Q_BLOCK = 128
NEG_INF = -1e30

MLA_HEADS = 8
Q_LORA = 256
KV_LORA = 128
MLA_NOPE = 64
MLA_ROPE = 32
MLA_QK = MLA_NOPE + MLA_ROPE
MLA_V = 64
MLA_W = MLA_HEADS * MLA_V

DIFF_HEADS = 4
DIFF_D = 32
DIFF_V = 2 * DIFF_D
DIFF_W = DIFF_HEADS * DIFF_V

MEM_HEADS = 4
MEM_D = 64
MEM_W = MEM_HEADS * MEM_D

D_MIX = MLA_W + DIFF_W + MEM_W

IN_SPLITS = [
    Q_LORA,
    KV_LORA,
    MLA_ROPE,
    DIFF_HEADS * 2 * DIFF_D,
    DIFF_HEADS * 2 * DIFF_D,
    DIFF_W,
    MEM_W,
    D_MIX,
]
D_IN = int(sum(IN_SPLITS))
IN_OFFSETS = [int(v) for v in np.cumsum(IN_SPLITS)[:-1]]

kernel_name = "hymba_mla_diffattn_memory_block"


def rms_norm(x, g, eps=RMS_EPS):
    xf = x.astype(jnp.float32)
    y = xf * lax.rsqrt(jnp.mean(xf * xf, axis=-1, keepdims=True) + eps)
    return (y * g.astype(jnp.float32)).astype(x.dtype)


def rope(x, pos):
    d = x.shape[-1]
    inv = ROPE_THETA ** (-jnp.arange(0, d, 2, dtype=jnp.float32) / d)
    ang = pos.astype(jnp.float32)[..., None] * inv
    cos = jnp.cos(ang)[:, :, None, :]
    sin = jnp.sin(ang)[:, :, None, :]
    xf = x.astype(jnp.float32)
    x1, x2 = xf[..., : d // 2], xf[..., d // 2:]
    out = jnp.concatenate([x1 * cos - x2 * sin, x2 * cos + x1 * sin], axis=-1)
    return out.astype(x.dtype)


def causal_mask(start, s_len):
    q_idx = start + jnp.arange(Q_BLOCK, dtype=jnp.int32)
    return jnp.arange(s_len, dtype=jnp.int32)[None, :] <= q_idx[:, None]


def causal_block_sweep(block_fn, q):
    b, s = q.shape[0], q.shape[1]
    nb = s // Q_BLOCK
    qb = jnp.moveaxis(q.reshape((b, nb, Q_BLOCK) + q.shape[2:]), 1, 0)
    starts = jnp.arange(nb, dtype=jnp.int32) * Q_BLOCK
    out = lax.map(lambda a: block_fn(a[0], a[1]), (qb, starts))
    return jnp.moveaxis(out, 0, 1).reshape((b, s) + out.shape[3:])


def mla_branch(cq, ckv, kr, pos, q_norm_g, kv_norm_g, w_uq, w_ukv, qn_g, kn_g):
    b, s, _ = cq.shape
    q = (rms_norm(cq, q_norm_g) @ w_uq).reshape(b, s, MLA_HEADS, MLA_QK)
    kv = (rms_norm(ckv, kv_norm_g) @ w_ukv).reshape(b, s, MLA_HEADS, MLA_NOPE + MLA_V)
    k_nope, v = kv[..., :MLA_NOPE], kv[..., MLA_NOPE:]
    k_rope = jnp.broadcast_to(kr.reshape(b, s, 1, MLA_ROPE), (b, s, MLA_HEADS, MLA_ROPE))
    k = jnp.concatenate([k_nope, k_rope], axis=-1)
    q = rms_norm(q, qn_g)
    k = rms_norm(k, kn_g)
    q = jnp.concatenate([q[..., :MLA_NOPE], rope(q[..., MLA_NOPE:], pos)], axis=-1)
    k = jnp.concatenate([k[..., :MLA_NOPE], rope(k[..., MLA_NOPE:], pos)], axis=-1)
    scale = 1.0 / math.sqrt(MLA_QK)

    def block_fn(qblk, start):
        sc = jnp.einsum('bqhd,bkhd->bhqk', qblk, k).astype(jnp.float32) * scale
        sc = jnp.where(causal_mask(start, s)[None, None], sc, NEG_INF)
        p = jax.nn.softmax(sc, axis=-1).astype(v.dtype)
        return jnp.einsum('bhqk,bkhd->bqhd', p, v)

    o = causal_block_sweep(block_fn, q)
    return o.reshape(b, s, MLA_W)


def diff_branch(dq, dk, dv, pos, qn_g, kn_g, lam_vecs, subln_g, lam_init):
    b, s, _ = dq.shape
    q = rms_norm(dq.reshape(b, s, DIFF_HEADS * 2, DIFF_D), qn_g)
    k = rms_norm(dk.reshape(b, s, DIFF_HEADS * 2, DIFF_D), kn_g)
    q = rope(q, pos).reshape(b, s, DIFF_HEADS, 2, DIFF_D)
    k = rope(k, pos).reshape(b, s, DIFF_HEADS, 2, DIFF_D)
    v = dv.reshape(b, s, DIFF_HEADS, DIFF_V)
    lv = lam_vecs.astype(jnp.float32)
    lam = jnp.exp(jnp.sum(lv[0] * lv[1])) - jnp.exp(jnp.sum(lv[2] * lv[3])) + lam_init
    scale = 1.0 / math.sqrt(DIFF_D)

    def block_fn(qblk, start):
        sc = jnp.einsum('bqhmd,bkhmd->bhmqk', qblk, k).astype(jnp.float32) * scale
        sc = jnp.where(causal_mask(start, s)[None, None, None], sc, NEG_INF)
        p = jax.nn.softmax(sc, axis=-1)
        a = (p[:, :, 0] - lam * p[:, :, 1]).astype(v.dtype)
        return jnp.einsum('bhqk,bkhd->bqhd', a, v)

    o = causal_block_sweep(block_fn, q)
    o = rms_norm(o, subln_g) * (1.0 - lam_init)
    return o.reshape(b, s, DIFF_W)


def mem_branch(mq, mem, mem_norm_g, w_mem_kv, qn_g, kn_g):
    b, s, _ = mq.shape
    m = mem.shape[1]
    kv = rms_norm(mem, mem_norm_g) @ w_mem_kv
    k = rms_norm(kv[..., :MEM_W].reshape(b, m, MEM_HEADS, MEM_D), kn_g)
    v = kv[..., MEM_W:].reshape(b, m, MEM_HEADS, MEM_D)
    q = rms_norm(mq.reshape(b, s, MEM_HEADS, MEM_D), qn_g)
    sc = jnp.einsum('bshd,bmhd->bhsm', q, k).astype(jnp.float32) / math.sqrt(MEM_D)
    p = jax.nn.softmax(sc, axis=-1).astype(v.dtype)
    o = jnp.einsum('bhsm,bmhd->bshd', p, v)
    return o.reshape(b, s, MEM_W)


def setup_inputs(seed: int = 0) -> dict:
    key = jax.random.key(seed)
    ks = jax.random.split(key, 24)
    f32 = jnp.float32

    def nrm(k, shape, fan_in):
        return jax.random.normal(k, shape, f32) * (fan_in ** -0.5)

    def gain(k, shape):
        return 1.0 + 0.01 * jax.random.normal(k, shape, f32)

    x = jax.random.normal(ks[0], (BATCH, SEQ, D_MODEL), f32)
    mem = jax.random.normal(ks[1], (BATCH, MEM_TOKENS, D_MODEL), f32)
    offsets = jax.random.randint(ks[2], (BATCH, 1), 0, 4096, dtype=jnp.int32)
    positions = offsets + jnp.arange(SEQ, dtype=jnp.int32)[None, :]
    return {
        "x": x,
        "mem": mem,
        "positions": positions,
        "norm_g": gain(ks[3], (DEPTH, D_MODEL)),
        "w_in": nrm(ks[4], (DEPTH, D_MODEL, D_IN), D_MODEL),
        "mla_q_norm_g": gain(ks[5], (DEPTH, Q_LORA)),
        "mla_kv_norm_g": gain(ks[6], (DEPTH, KV_LORA)),
        "w_uq": nrm(ks[7], (DEPTH, Q_LORA, MLA_HEADS * MLA_QK), Q_LORA),
        "w_ukv": nrm(ks[8], (DEPTH, KV_LORA, MLA_HEADS * (MLA_NOPE + MLA_V)), KV_LORA),
        "mla_qn_g": gain(ks[9], (DEPTH, MLA_QK)),
        "mla_kn_g": gain(ks[10], (DEPTH, MLA_QK)),
        "diff_qn_g": gain(ks[11], (DEPTH, DIFF_D)),
        "diff_kn_g": gain(ks[12], (DEPTH, DIFF_D)),
        "diff_lambda": 0.1 * jax.random.normal(ks[13], (DEPTH, 4, DIFF_D), f32),
        "diff_subln_g": gain(ks[14], (DEPTH, DIFF_V)),
        "mem_norm_g": gain(ks[15], (DEPTH, D_MODEL)),
        "w_mem_kv": nrm(ks[16], (DEPTH, D_MODEL, 2 * MEM_W), D_MODEL),
        "mem_qn_g": gain(ks[17], (DEPTH, MEM_D)),
        "mem_kn_g": gain(ks[18], (DEPTH, MEM_D)),
        "w_out": nrm(ks[19], (DEPTH, D_MIX, D_MODEL), D_MIX),
    }


def reference(x, mem, positions, norm_g, w_in, mla_q_norm_g, mla_kv_norm_g, w_uq, w_ukv,
              mla_qn_g, mla_kn_g, diff_qn_g, diff_kn_g, diff_lambda, diff_subln_g,
              mem_norm_g, w_mem_kv, mem_qn_g, mem_kn_g, w_out):
    for l in range(DEPTH):
        lam_init = 0.8 - 0.6 * math.exp(-0.3 * l)
        h = rms_norm(x, norm_g[l])
        proj = h @ w_in[l]
        cq, ckv, kr, dq, dk, dv, mq, z = jnp.split(proj, IN_OFFSETS, axis=-1)
        y_mla = mla_branch(cq, ckv, kr, positions, mla_q_norm_g[l], mla_kv_norm_g[l],
                           w_uq[l], w_ukv[l], mla_qn_g[l], mla_kn_g[l])
        y_diff = diff_branch(dq, dk, dv, positions, diff_qn_g[l], diff_kn_g[l],
                             diff_lambda[l], diff_subln_g[l], lam_init)
        y_mem = mem_branch(mq, mem, mem_norm_g[l], w_mem_kv[l], mem_qn_g[l], mem_kn_g[l])
        y = jnp.concatenate([y_mla, y_diff, y_mem], axis=-1) * jax.nn.silu(z)
        x = x + y @ w_out[l]
    return x
```

```python
import functools
import math

import jax
import jax.numpy as jnp
from jax import lax
from jax.experimental import pallas as pl
from jax.experimental.pallas import tpu as pltpu

F32 = jnp.float32
BF16 = jnp.bfloat16

D_MODEL = 1024
ROPE_THETA = 10000.0
RMS_EPS = 1e-6
NEG_INF = -1e30

MLA_HEADS = 8
Q_LORA = 256
KV_LORA = 128
MLA_NOPE = 64
MLA_ROPE = 32
MLA_QK = MLA_NOPE + MLA_ROPE
MLA_V = 64
MLA_W = MLA_HEADS * MLA_V

DIFF_HEADS = 4
DIFF_D = 32
DIFF_V = 2 * DIFF_D
DIFF_W = DIFF_HEADS * DIFF_V

MEM_HEADS = 4
MEM_D = 64
MEM_W = MEM_HEADS * MEM_D

D_MIX = MLA_W + DIFF_W + MEM_W

LANES = 128
ROPE_D = 32
HEAD_PAD = LANES

_C_CQ = 0
_C_CKV = _C_CQ + Q_LORA
_C_KR = _C_CKV + KV_LORA
_C_DQ = _C_KR + LANES
_C_DK = _C_DQ + 2 * DIFF_HEADS * DIFF_D
_C_DV = _C_DK + 2 * DIFF_HEADS * DIFF_D
_C_MQ = _C_DV + DIFF_HEADS * LANES
_C_Z = _C_MQ + MEM_W
_C_END = _C_Z + D_MIX

VMEM_LIMIT = 56 * 1024 * 1024

TM_PROJ = 512
TM_OUT = 512
TQ_MLA = 512
TK_MLA = 512
TQ_DIFF = 512
TK_DIFF = 512

_NT = (((1,), (1,)), ((), ()))


def _rms(x):
    return x * lax.rsqrt(jnp.mean(x * x, axis=-1, keepdims=True) + RMS_EPS)


def _lane_iota(shape):
    return lax.broadcasted_iota(jnp.int32, shape, len(shape) - 1)


def _seg_mean_sq(x, seg, count):
    sq = x * x
    if seg == LANES:
        return jnp.sum(sq, axis=-1, keepdims=True) * (1.0 / count)
    lane = _lane_iota((1, LANES))
    out = jnp.zeros_like(x)
    for j in range(LANES // seg):
        mask = ((lane // seg) == j).astype(F32)
        out = out + jnp.sum(sq * mask, axis=-1, keepdims=True) * mask
    return out * (1.0 / count)


def _swap_half(x):
    lane = _lane_iota(x.shape)
    up = pltpu.roll(x, LANES - ROPE_D // 2, 1)
    dn = pltpu.roll(x, ROPE_D // 2, 1)
    return jnp.where((lane & (ROPE_D // 2)) == 0, up, dn)


def _rope(x, cos, sin_signed):
    return x * cos + _swap_half(x) * sin_signed


def _rope_table_kernel(pos_ref, inv_ref, cos_ref, sin_ref):
    ang = pos_ref[...].astype(F32) * inv_ref[...]
    lane = _lane_iota(ang.shape)
    sign = jnp.where((lane & (ROPE_D // 2)) == 0, -1.0, 1.0)
    cos_ref[...] = jnp.cos(ang)
    sin_ref[...] = jnp.sin(ang) * sign


def _rope_tables(positions):
    n = positions.size
    tm = 1024
    inv = ROPE_THETA ** (-jnp.arange(0, ROPE_D, 2, dtype=F32) / ROPE_D)
    inv_t = jnp.tile(inv, LANES // (ROPE_D // 2))[None, :]
    pos = positions.reshape(n, 1)
    return pl.pallas_call(
        _rope_table_kernel,
        grid=(n // tm,),
        in_specs=[pl.BlockSpec((tm, 1), lambda i: (i, 0)),
                  pl.BlockSpec((1, LANES), lambda i: (0, 0))],
        out_specs=[pl.BlockSpec((tm, LANES), lambda i: (i, 0)),
                   pl.BlockSpec((tm, LANES), lambda i: (i, 0))],
        out_shape=[jax.ShapeDtypeStruct((n, LANES), F32)] * 2,
        name="rope_tables",
    )(pos, inv_t)


def _mem_kv_kernel(mem_ref, g_ref, w_ref, kn_ref, k_ref, v_ref):
    m = mem_ref[0]
    mn = (_rms(m) * g_ref[0]).astype(BF16)
    kv = jnp.dot(mn, w_ref[0], preferred_element_type=F32)
    for c in range(MEM_W // LANES):
        sl = slice(c * LANES, (c + 1) * LANES)
        kc = kv[:, sl]
        kc = kc * lax.rsqrt(_seg_mean_sq(kc, MEM_D, MEM_D) + RMS_EPS) * kn_ref[0][:, sl]
        k_ref[0, 0, :, sl] = kc.astype(BF16)
    v_ref[0, 0] = kv[:, MEM_W:].astype(BF16)


def _mem_kv(mem, mem_norm_g, w_mem_kv, mem_kn_g):
    depth = w_mem_kv.shape[0]
    b, m, d = mem.shape
    kn_t = jnp.tile(mem_kn_g, (1, MEM_HEADS))[:, None, :]
    return pl.pallas_call(
        _mem_kv_kernel,
        grid=(depth, b),
        in_specs=[pl.BlockSpec((1, m, d), lambda l, i: (i, 0, 0)),
                  pl.BlockSpec((1, 1, d), lambda l, i: (l, 0, 0)),
                  pl.BlockSpec((1, d, 2 * MEM_W), lambda l, i: (l, 0, 0)),
                  pl.BlockSpec((1, 1, MEM_W), lambda l, i: (l, 0, 0))],
        out_specs=[pl.BlockSpec((1, 1, m, MEM_W), lambda l, i: (l, i, 0, 0)),
                   pl.BlockSpec((1, 1, m, MEM_W), lambda l, i: (l, i, 0, 0))],
        out_shape=[jax.ShapeDtypeStruct((depth, b, m, MEM_W), BF16)] * 2,
        name="mem_kv",
    )(mem, mem_norm_g[:, None, :], w_mem_kv.astype(BF16), kn_t)


def _proj_kernel(x_ref, cos_ref, sin_ref, gx_ref, win_ref, gcq_ref, gckv_ref, wuq_ref, wuk_ref, wuv_ref,
                 qn_ref, kn_ref, dqn_ref, dkn_ref, mqn_ref, kmem_ref, vmem_ref,
                 q_out, k_out, v_out, dq_out, dk_out, dv_out, ymem_out, gate_out):
    h = (_rms(x_ref[...]) * gx_ref[...]).astype(BF16)
    proj = jnp.dot(h, win_ref[...], preferred_element_type=F32)

    cos = cos_ref[...]
    sin = sin_ref[...]
    lane = _lane_iota((1, LANES))
    is_rope = (lane >= MLA_NOPE) & (lane < MLA_QK)
    cos_m = jnp.where(is_rope, cos, 1.0)
    sin_m = jnp.where(is_rope, sin, 0.0)

    cqn = (_rms(proj[:, _C_CQ:_C_CQ + Q_LORA]) * gcq_ref[...]).astype(BF16)
    q = jnp.dot(cqn, wuq_ref[...], preferred_element_type=F32)
    q_scale = 1.0 / math.sqrt(MLA_QK)
    for hd in range(MLA_HEADS):
        sl = slice(hd * HEAD_PAD, (hd + 1) * HEAD_PAD)
        qh = q[:, sl]
        qh = qh * lax.rsqrt(_seg_mean_sq(qh, LANES, MLA_QK) + RMS_EPS) * qn_ref[...]
        q_out[:, sl] = (_rope(qh, cos_m, sin_m) * q_scale).astype(BF16)

    ckvn = (_rms(proj[:, _C_CKV:_C_CKV + KV_LORA]) * gckv_ref[...]).astype(BF16)
    k_nope = jnp.dot(ckvn, wuk_ref[...], preferred_element_type=F32)
    v_out[...] = jnp.dot(ckvn, wuv_ref[...], preferred_element_type=F32).astype(BF16)
    kr = proj[:, _C_KR:_C_KR + LANES]
    for hd in range(MLA_HEADS):
        sl = slice(hd * HEAD_PAD, (hd + 1) * HEAD_PAD)
        kh = k_nope[:, sl] + kr
        kh = kh * lax.rsqrt(_seg_mean_sq(kh, LANES, MLA_QK) + RMS_EPS) * kn_ref[...]
        k_out[:, sl] = _rope(kh, cos_m, sin_m).astype(BF16)

    d_scale = 1.0 / math.sqrt(DIFF_D)
    for c in range(2 * DIFF_HEADS * DIFF_D // LANES):
        sl = slice(c * LANES, (c + 1) * LANES)
        xq = proj[:, _C_DQ + c * LANES:_C_DQ + (c + 1) * LANES]
        xq = xq * lax.rsqrt(_seg_mean_sq(xq, DIFF_D, DIFF_D) + RMS_EPS) * dqn_ref[:, sl]
        dq_out[:, sl] = (_rope(xq, cos, sin) * d_scale).astype(BF16)
        xk = proj[:, _C_DK + c * LANES:_C_DK + (c + 1) * LANES]
        xk = xk * lax.rsqrt(_seg_mean_sq(xk, DIFF_D, DIFF_D) + RMS_EPS) * dkn_ref[:, sl]
        dk_out[:, sl] = _rope(xk, cos, sin).astype(BF16)
    dv_out[...] = proj[:, _C_DV:_C_DV + DIFF_HEADS * LANES].astype(BF16)

    z = proj[:, _C_Z:_C_END]
    gate = z * (1.0 / (1.0 + jnp.exp(-z)))
    gate_out[...] = gate[:, :MLA_W + DIFF_W].astype(BF16)

    m_scale = 1.0 / math.sqrt(MEM_D)
    mq_parts = []
    for c in range(MEM_W // LANES):
        xm = proj[:, _C_MQ + c * LANES:_C_MQ + (c + 1) * LANES]
        xm = xm * lax.rsqrt(_seg_mean_sq(xm, MEM_D, MEM_D) + RMS_EPS) * mqn_ref[:, c * LANES:(c + 1) * LANES]
        mq_parts.append(xm * m_scale)
    mq = jnp.concatenate(mq_parts, axis=1)
    kmem = kmem_ref[0]
    vmem = vmem_ref[0]
    lane_w = _lane_iota((1, MEM_W))
    o = jnp.zeros(mq.shape, F32)
    for hd in range(MEM_HEADS):
        head_mask = (lane_w // MEM_D) == hd
        qh = jnp.where(head_mask, mq, 0.0).astype(BF16)
        s = lax.dot_general(qh, kmem, _NT, preferred_element_type=F32)
        p = jnp.exp(s - jnp.max(s, axis=-1, keepdims=True))
        inv_l = 1.0 / jnp.sum(p, axis=-1, keepdims=True)
        vh = jnp.where(head_mask, vmem, jnp.zeros_like(vmem))
        o = o + jnp.dot(p.astype(BF16), vh, preferred_element_type=F32) * inv_l
    ymem_out[...] = (o * gate[:, MLA_W + DIFF_W:]).astype(BF16)


def _proj_call(x2d, cos_t, sin_t, kmem, vmem, p, seq):
    n = x2d.shape[0]
    tm = TM_PROJ
    per_batch = seq // tm
    row = lambda i: (i, 0)
    const = lambda i: (0, 0)
    full = lambda a: pl.BlockSpec(a.shape, const)
    mem_spec = pl.BlockSpec((1,) + kmem.shape[1:], lambda i: (i // per_batch, 0, 0))
    in_specs = [pl.BlockSpec((tm, D_MODEL), row), pl.BlockSpec((tm, LANES), row), pl.BlockSpec((tm, LANES), row),
                full(p["gx"]), full(p["w_in"]), full(p["gcq"]), full(p["gckv"]),
                full(p["w_uq"]), full(p["w_uk"]), full(p["w_uv"]),
                full(p["qn"]), full(p["kn"]), full(p["dqn"]), full(p["dkn"]), full(p["mqn"]),
                mem_spec, mem_spec]
    widths = [MLA_HEADS * HEAD_PAD, MLA_HEADS * HEAD_PAD, MLA_HEADS * HEAD_PAD,
              2 * DIFF_HEADS * DIFF_D, 2 * DIFF_HEADS * DIFF_D, DIFF_HEADS * LANES, MEM_W, MLA_W + DIFF_W]
    return pl.pallas_call(
        _proj_kernel,
        grid=(n // tm,),
        in_specs=in_specs,
        out_specs=[pl.BlockSpec((tm, w), row) for w in widths],
        out_shape=[jax.ShapeDtypeStruct((n, w), BF16) for w in widths],
        compiler_params=pltpu.CompilerParams(dimension_semantics=("arbitrary",), vmem_limit_bytes=VMEM_LIMIT),
        name="proj_prep",
    )(x2d, cos_t, sin_t, p["gx"], p["w_in"], p["gcq"], p["gckv"], p["w_uq"], p["w_uk"], p["w_uv"],
      p["qn"], p["kn"], p["dqn"], p["dkn"], p["mqn"], kmem, vmem)


def _causal_keep(tq, tk, col_offset):
    r = lax.broadcasted_iota(jnp.int32, (tq, tk), 0)
    c = lax.broadcasted_iota(jnp.int32, (tq, tk), 1)
    return (c + col_offset) <= r


def _head_lanes(a0, a1):
    lane = _lane_iota((1, LANES))
    return jnp.where(lane < LANES // 2, a0, a1)


def _softmax_step(s, m_ref, l_ref, idx):
    m_old = m_ref[idx]
    m_new = jnp.maximum(m_old, jnp.max(s, axis=-1, keepdims=True))
    alpha = jnp.exp(m_old - m_new)
    p = jnp.exp(s - m_new)
    l_ref[idx] = alpha * l_ref[idx] + jnp.sum(p, axis=-1, keepdims=True)
    m_ref[idx] = m_new
    return p, alpha


def _mla_attn_kernel(q_ref, k_ref, v_ref, gate_ref, o_ref, m_ref, l_ref, acc_ref, *, tq, tk):
    qi = pl.program_id(2)
    m_ref[...] = jnp.full(m_ref.shape, NEG_INF, F32)
    l_ref[...] = jnp.zeros(l_ref.shape, F32)
    acc_ref[...] = jnp.zeros(acc_ref.shape, F32)

    def block(start, keep):
        rows = pl.ds(start, tk)
        alphas = []
        pv = None
        for hd in range(2):
            sl = slice(hd * HEAD_PAD, (hd + 1) * HEAD_PAD)
            s = lax.dot_general(q_ref[:, sl], k_ref[rows, sl], _NT, preferred_element_type=F32)
            if keep is not None:
                s = jnp.where(keep, s, NEG_INF)
            p, alpha = _softmax_step(s, m_ref, l_ref, hd)
            alphas.append(alpha)
            d = jnp.dot(p.astype(BF16), v_ref[rows, sl], preferred_element_type=F32)
            pv = d if pv is None else pv + d
        acc_ref[...] = acc_ref[...] * _head_lanes(alphas[0], alphas[1]) + pv

    n_full = qi * (tq // tk)

    def body(j, carry):
        block(pl.multiple_of(j * tk, tk), None)
        return carry

    lax.fori_loop(0, n_full, body, 0)
    for d in range(tq // tk):
        block(pl.multiple_of(qi * tq + d * tk, tk), _causal_keep(tq, tk, d * tk))

    inv = _head_lanes(1.0 / l_ref[0], 1.0 / l_ref[1])
    o_ref[...] = (acc_ref[...] * inv * gate_ref[...].astype(F32)).astype(BF16)


def _mla_attn_call(q, k, v, gate, batch, seq):
    tq, tk = TQ_MLA, TK_MLA
    nq = seq // tq
    pairs = MLA_HEADS // 2
    qmap = lambda b, hp, qi: (b * nq + qi, hp)
    kvmap = lambda b, hp, qi: (b, hp)
    return pl.pallas_call(
        functools.partial(_mla_attn_kernel, tq=tq, tk=tk),
        grid=(batch, pairs, nq),
        in_specs=[pl.BlockSpec((tq, 2 * HEAD_PAD), qmap),
                  pl.BlockSpec((seq, 2 * HEAD_PAD), kvmap),
                  pl.BlockSpec((seq, 2 * HEAD_PAD), kvmap),
                  pl.BlockSpec((tq, LANES), qmap)],
        out_specs=pl.BlockSpec((tq, LANES), qmap),
        out_shape=jax.ShapeDtypeStruct((batch * seq, MLA_W), BF16),
        scratch_shapes=[pltpu.VMEM((2, tq, 1), F32), pltpu.VMEM((2, tq, 1), F32), pltpu.VMEM((tq, LANES), F32)],
        compiler_params=pltpu.CompilerParams(dimension_semantics=("arbitrary",) * 3, vmem_limit_bytes=VMEM_LIMIT),
        name="mla_attn",
    )(q, k, v, gate)


def _diff_attn_kernel(q_ref, k_ref, v_ref, gate_ref, lam_ref, sg_ref, o_ref, qm_ref, m_ref, l_ref, acc_ref,
                      *, tq, tk, lam_init):
    qi = pl.program_id(2)
    lane = _lane_iota((1, LANES))
    q = q_ref[...]
    for i in range(4):
        qm_ref[i] = jnp.where((lane // DIFF_D) == i, q, jnp.zeros_like(q))
    m_ref[...] = jnp.full(m_ref.shape, NEG_INF, F32)
    l_ref[...] = jnp.zeros(l_ref.shape, F32)
    acc_ref[...] = jnp.zeros(acc_ref.shape, F32)

    def block(start, keep):
        rows = pl.ds(start, tk)
        kc = k_ref[rows, :]
        for mp in range(2):
            alphas = []
            pv = None
            for hd in range(2):
                idx = 2 * hd + mp
                s = lax.dot_general(qm_ref[idx], kc, _NT, preferred_element_type=F32)
                if keep is not None:
                    s = jnp.where(keep, s, NEG_INF)
                p, alpha = _softmax_step(s, m_ref, l_ref, idx)
                alphas.append(alpha)
                d = jnp.dot(p.astype(BF16), v_ref[rows, hd * LANES:(hd + 1) * LANES], preferred_element_type=F32)
                pv = d if pv is None else pv + d
            acc_ref[mp] = acc_ref[mp] * _head_lanes(alphas[0], alphas[1]) + pv

    n_full = qi * (tq // tk)

    def body(j, carry):
        block(pl.multiple_of(j * tk, tk), None)
        return carry

    lax.fori_loop(0, n_full, body, 0)
    for d in range(tq // tk):
        block(pl.multiple_of(qi * tq + d * tk, tk), _causal_keep(tq, tk, d * tk))

    lv = lam_ref[...]
    lam = (jnp.exp(jnp.sum(lv[0:1] * lv[1:2], axis=-1, keepdims=True))
           - jnp.exp(jnp.sum(lv[2:3] * lv[3:4], axis=-1, keepdims=True)) + lam_init)
    o1 = acc_ref[0] * _head_lanes(1.0 / l_ref[0], 1.0 / l_ref[2])
    o2 = acc_ref[1] * _head_lanes(1.0 / l_ref[1], 1.0 / l_ref[3])
    o = o1 - lam * o2
    o = o * lax.rsqrt(_seg_mean_sq(o, DIFF_V, DIFF_V) + RMS_EPS) * sg_ref[...] * (1.0 - lam_init)
    o_ref[...] = (o * gate_ref[...].astype(F32)).astype(BF16)


def _diff_attn_call(q, k, v, gate, lam_pad, subln_t, batch, seq, lam_init):
    tq, tk = TQ_DIFF, TK_DIFF
    nq = seq // tq
    pairs = DIFF_HEADS // 2
    gate_off = MLA_W // LANES
    qmap = lambda b, hp, qi: (b * nq + qi, hp)
    kvmap = lambda b, hp, qi: (b, hp)
    const = lambda b, hp, qi: (0, 0)
    return pl.pallas_call(
        functools.partial(_diff_attn_kernel, tq=tq, tk=tk, lam_init=lam_init),
        grid=(batch, pairs, nq),
        in_specs=[pl.BlockSpec((tq, LANES), qmap),
                  pl.BlockSpec((seq, LANES), kvmap),
                  pl.BlockSpec((seq, 2 * LANES), kvmap),
                  pl.BlockSpec((tq, LANES), lambda b, hp, qi: (b * nq + qi, gate_off + hp)),
                  pl.BlockSpec(lam_pad.shape, const),
                  pl.BlockSpec(subln_t.shape, const)],
        out_specs=pl.BlockSpec((tq, LANES), qmap),
        out_shape=jax.ShapeDtypeStruct((batch * seq, DIFF_W), BF16),
        scratch_shapes=[pltpu.VMEM((4, tq, LANES), BF16), pltpu.VMEM((4, tq, 1), F32), pltpu.VMEM((4, tq, 1), F32),
                        pltpu.VMEM((2, tq, LANES), F32)],
        compiler_params=pltpu.CompilerParams(dimension_semantics=("arbitrary",) * 3, vmem_limit_bytes=VMEM_LIMIT),
        name="diff_attn",
    )(q, k, v, gate, lam_pad, subln_t)


def _out_kernel(x_ref, ymla_ref, ydiff_ref, ymem_ref, w_ref, o_ref):
    y = jnp.concatenate([ymla_ref[...], ydiff_ref[...], ymem_ref[...]], axis=1)
    o_ref[...] = x_ref[...] + jnp.dot(y, w_ref[...], preferred_element_type=F32)


def _out_call(x2d, y_mla, y_diff, y_mem, w_out):
    n = x2d.shape[0]
    tm = TM_OUT
    row = lambda i: (i, 0)
    return pl.pallas_call(
        _out_kernel,
        grid=(n // tm,),
        in_specs=[pl.BlockSpec((tm, D_MODEL), row), pl.BlockSpec((tm, MLA_W), row),
                  pl.BlockSpec((tm, DIFF_W), row), pl.BlockSpec((tm, MEM_W), row),
                  pl.BlockSpec(w_out.shape, lambda i: (0, 0))],
        out_specs=pl.BlockSpec((tm, D_MODEL), row),
        out_shape=jax.ShapeDtypeStruct((n, D_MODEL), F32),
        compiler_params=pltpu.CompilerParams(dimension_semantics=("arbitrary",), vmem_limit_bytes=VMEM_LIMIT),
        name="out_proj",
    )(x2d, y_mla, y_diff, y_mem, w_out)


def _pad_cols(a, width):
    return jnp.pad(a, ((0, 0), (0, width - a.shape[1])))


def _pair_placed(a, heads, width):
    rows = a.shape[0]
    a3 = a.reshape(rows, heads, width)
    z = jnp.zeros_like(a3)
    even = jnp.concatenate([a3, z], axis=-1)
    odd = jnp.concatenate([z, a3], axis=-1)
    is_even = (jnp.arange(heads) % 2 == 0)[None, :, None]
    return jnp.where(is_even, even, odd).reshape(rows, heads * 2 * width)


def _layer_params(l, norm_g, w_in, mla_q_norm_g, mla_kv_norm_g, w_uq, w_ukv, mla_qn_g, mla_kn_g,
                  diff_qn_g, diff_kn_g, mem_qn_g):
    w = w_in[l]
    o_cq, o_ckv, o_kr = 0, Q_LORA, Q_LORA + KV_LORA
    o_dq = o_kr + MLA_ROPE
    o_dk = o_dq + 2 * DIFF_HEADS * DIFF_D
    o_dv = o_dk + 2 * DIFF_HEADS * DIFF_D
    o_mq = o_dv + DIFF_W
    o_z = o_mq + MEM_W
    zeros = lambda c: jnp.zeros((w.shape[0], c), w.dtype)
    w_in_p = jnp.concatenate([
        w[:, o_cq:o_ckv], w[:, o_ckv:o_kr],
        zeros(MLA_NOPE), w[:, o_kr:o_dq], zeros(LANES - MLA_QK),
        w[:, o_dq:o_dk], w[:, o_dk:o_dv],
        _pair_placed(w[:, o_dv:o_mq], DIFF_HEADS, DIFF_V),
        w[:, o_mq:o_z], w[:, o_z:]], axis=1).astype(BF16)
    w_uq_p = jnp.pad(w_uq[l].reshape(Q_LORA, MLA_HEADS, MLA_QK),
                     ((0, 0), (0, 0), (0, HEAD_PAD - MLA_QK))).reshape(Q_LORA, MLA_HEADS * HEAD_PAD).astype(BF16)
    kv3 = w_ukv[l].reshape(KV_LORA, MLA_HEADS, MLA_NOPE + MLA_V)
    w_uk_p = jnp.pad(kv3[:, :, :MLA_NOPE],
                     ((0, 0), (0, 0), (0, HEAD_PAD - MLA_NOPE))).reshape(KV_LORA, MLA_HEADS * HEAD_PAD).astype(BF16)
    w_uv_p = _pair_placed(kv3[:, :, MLA_NOPE:].reshape(KV_LORA, MLA_HEADS * MLA_V), MLA_HEADS, MLA_V).astype(BF16)
    return {
        "gx": norm_g[l][None, :],
        "w_in": w_in_p,
        "gcq": mla_q_norm_g[l][None, :],
        "gckv": mla_kv_norm_g[l][None, :],
        "w_uq": w_uq_p, "w_uk": w_uk_p, "w_uv": w_uv_p,
        "qn": _pad_cols(mla_qn_g[l][None, :], HEAD_PAD),
        "kn": _pad_cols(mla_kn_g[l][None, :], HEAD_PAD),
        "dqn": jnp.tile(diff_qn_g[l], 2 * DIFF_HEADS)[None, :],
        "dkn": jnp.tile(diff_kn_g[l], 2 * DIFF_HEADS)[None, :],
        "mqn": jnp.tile(mem_qn_g[l], MEM_HEADS)[None, :],
    }


def kernel(x, mem, positions, norm_g, w_in, mla_q_norm_g, mla_kv_norm_g, w_uq, w_ukv, mla_qn_g, mla_kn_g,
           diff_qn_g, diff_kn_g, diff_lambda, diff_subln_g, mem_norm_g, w_mem_kv, mem_qn_g, mem_kn_g, w_out):
    batch, seq, d_model = x.shape
    depth = w_in.shape[0]
    assert d_model == D_MODEL and seq % max(TM_PROJ, TQ_MLA, TQ_DIFF, TM_OUT) == 0 and (batch * seq) % 1024 == 0
    cos_t, sin_t = _rope_tables(positions)
    kmem_all, vmem_all = _mem_kv(mem, mem_norm_g, w_mem_kv, mem_kn_g)
    x2d = x.reshape(batch * seq, D_MODEL)
    for l in range(depth):
        lam_init = 0.8 - 0.6 * math.exp(-0.3 * l)
        p = _layer_params(l, norm_g, w_in, mla_q_norm_g, mla_kv_norm_g, w_uq, w_ukv, mla_qn_g, mla_kn_g,
                          diff_qn_g, diff_kn_g, mem_qn_g)
        q, k, v, dq, dk, dv, y_mem, gate = _proj_call(x2d, cos_t, sin_t, kmem_all[l], vmem_all[l], p, seq)
        y_mla = _mla_attn_call(q, k, v, gate, batch, seq)
        lam_pad = jnp.pad(diff_lambda[l], ((0, 4), (0, LANES - DIFF_D)))
        subln_t = jnp.tile(diff_subln_g[l], 2)[None, :]
        y_diff = _diff_attn_call(dq, dk, dv, gate, lam_pad, subln_t, batch, seq, lam_init)
        x2d = _out_call(x2d, y_mla, y_diff, y_mem, w_out[l].astype(BF16))
    return x2d.reshape(batch, seq, D_MODEL)
```

```python
import functools
import math

import jax
import jax.numpy as jnp
from jax import lax
from jax.experimental import pallas as pl
from jax.experimental.pallas import tpu as pltpu

F32 = jnp.float32
BF16 = jnp.bfloat16

D_MODEL = 1024
ROPE_THETA = 10000.0
RMS_EPS = 1e-6
NEG_INF = -1e30
LOG2_E = math.log2(math.e)

MLA_HEADS = 8
Q_LORA = 256
KV_LORA = 128
MLA_NOPE = 64
MLA_ROPE = 32
MLA_QK = MLA_NOPE + MLA_ROPE
MLA_V = 64
MLA_W = MLA_HEADS * MLA_V

DIFF_HEADS = 4
DIFF_D = 32
DIFF_V = 2 * DIFF_D
DIFF_W = DIFF_HEADS * DIFF_V

MEM_HEADS = 4
MEM_D = 64
MEM_W = MEM_HEADS * MEM_D

D_MIX = MLA_W + DIFF_W + MEM_W

LANES = 128
ROPE_D = 32
HEAD_PAD = LANES

_C_CQ = 0
_C_CKV = _C_CQ + Q_LORA
_C_KR = _C_CKV + KV_LORA
_C_DQ = _C_KR + LANES
_C_DK = _C_DQ + 2 * DIFF_HEADS * DIFF_D
_C_DV = _C_DK + 2 * DIFF_HEADS * DIFF_D
_C_MQ = _C_DV + DIFF_HEADS * LANES
_C_Z = _C_MQ + MEM_W
_C_END = _C_Z + D_MIX

VMEM_LIMIT = 56 * 1024 * 1024

V_DIM = MLA_V
VT_ROWS = 80

TK = 512
TM_PROJ = TK
TM_OUT = 512
TQ_MLA = 512
TQ_DIFF = 512

_NT = (((1,), (1,)), ((), ()))


def _rms(x):
    return x * lax.rsqrt(jnp.mean(x * x, axis=-1, keepdims=True) + RMS_EPS)


def _lane_iota(shape):
    return lax.broadcasted_iota(jnp.int32, shape, len(shape) - 1)


def _seg_mean_sq(x, seg, count):
    sq = x * x
    if seg == LANES:
        return jnp.sum(sq, axis=-1, keepdims=True) * (1.0 / count)
    lane = _lane_iota((1, LANES))
    out = jnp.zeros_like(x)
    for j in range(LANES // seg):
        mask = ((lane // seg) == j).astype(F32)
        out = out + jnp.sum(sq * mask, axis=-1, keepdims=True) * mask
    return out * (1.0 / count)


def _swap_half(x):
    lane = _lane_iota(x.shape)
    up = pltpu.roll(x, LANES - ROPE_D // 2, 1)
    dn = pltpu.roll(x, ROPE_D // 2, 1)
    return jnp.where((lane & (ROPE_D // 2)) == 0, up, dn)


def _rope(x, cos, sin_signed):
    return x * cos + _swap_half(x) * sin_signed


def _rope_table_kernel(pos_ref, inv_ref, cos_ref, sin_ref):
    ang = pos_ref[...].astype(F32) * inv_ref[...]
    lane = _lane_iota(ang.shape)
    sign = jnp.where((lane & (ROPE_D // 2)) == 0, -1.0, 1.0)
    cos_ref[...] = jnp.cos(ang)
    sin_ref[...] = jnp.sin(ang) * sign


def _rope_tables(positions):
    n = positions.size
    tm = 1024
    inv = ROPE_THETA ** (-jnp.arange(0, ROPE_D, 2, dtype=F32) / ROPE_D)
    inv_t = jnp.tile(inv, LANES // (ROPE_D // 2))[None, :]
    pos = positions.reshape(n, 1)
    return pl.pallas_call(
        _rope_table_kernel,
        grid=(n // tm,),
        in_specs=[pl.BlockSpec((tm, 1), lambda i: (i, 0)),
                  pl.BlockSpec((1, LANES), lambda i: (0, 0))],
        out_specs=[pl.BlockSpec((tm, LANES), lambda i: (i, 0)),
                   pl.BlockSpec((tm, LANES), lambda i: (i, 0))],
        out_shape=[jax.ShapeDtypeStruct((n, LANES), F32)] * 2,
        name="rope_tables",
    )(pos, inv_t)


def _mem_kv_kernel(mem_ref, g_ref, w_ref, kn_ref, k_ref, v_ref):
    m = mem_ref[0]
    mn = (_rms(m) * g_ref[0]).astype(BF16)
    kv = jnp.dot(mn, w_ref[0], preferred_element_type=F32)
    for c in range(MEM_W // LANES):
        sl = slice(c * LANES, (c + 1) * LANES)
        kc = kv[:, sl]
        kc = kc * lax.rsqrt(_seg_mean_sq(kc, MEM_D, MEM_D) + RMS_EPS) * kn_ref[0][:, sl]
        k_ref[0, 0, :, sl] = kc.astype(BF16)
    v_ref[0, 0] = kv[:, MEM_W:].astype(BF16)


def _mem_kv(mem, mem_norm_g, w_mem_kv, mem_kn_g):
    depth = w_mem_kv.shape[0]
    b, m, d = mem.shape
    kn_t = jnp.tile(mem_kn_g, (1, MEM_HEADS))[:, None, :]
    return pl.pallas_call(
        _mem_kv_kernel,
        grid=(depth, b),
        in_specs=[pl.BlockSpec((1, m, d), lambda l, i: (i, 0, 0)),
                  pl.BlockSpec((1, 1, d), lambda l, i: (l, 0, 0)),
                  pl.BlockSpec((1, d, 2 * MEM_W), lambda l, i: (l, 0, 0)),
                  pl.BlockSpec((1, 1, MEM_W), lambda l, i: (l, 0, 0))],
        out_specs=[pl.BlockSpec((1, 1, m, MEM_W), lambda l, i: (l, i, 0, 0)),
                   pl.BlockSpec((1, 1, m, MEM_W), lambda l, i: (l, i, 0, 0))],
        out_shape=[jax.ShapeDtypeStruct((depth, b, m, MEM_W), BF16)] * 2,
        name="mem_kv",
    )(mem, mem_norm_g[:, None, :], w_mem_kv.astype(BF16), kn_t)


def _store_values_t(v, out_ref, heads):
    ones_row = (_lane_iota((1, LANES)) == V_DIM).astype(F32)
    for hd in range(heads):
        vt = (v[:, hd * LANES:(hd + 1) * LANES] + ones_row).T
        out_ref[0, hd * VT_ROWS:(hd + 1) * VT_ROWS, :] = vt[:VT_ROWS].astype(BF16)


def _proj_kernel(x_ref, cos_ref, sin_ref, gx_ref, win_ref, gcq_ref, gckv_ref, wuq_ref, wuk_ref, wuv_ref,
                 qn_ref, kn_ref, dqn_ref, dkn_ref, mqn_ref, kmem_ref, vmem_ref,
                 q_out, k_out, v_out, dq_out, dk_out, dv_out, ymem_out, gate_out):
    h = (_rms(x_ref[...]) * gx_ref[...]).astype(BF16)
    proj = jnp.dot(h, win_ref[...], preferred_element_type=F32)

    cos = cos_ref[...]
    sin = sin_ref[...]
    lane = _lane_iota((1, LANES))
    is_rope = (lane >= MLA_NOPE) & (lane < MLA_QK)
    cos_m = jnp.where(is_rope, cos, 1.0)
    sin_m = jnp.where(is_rope, sin, 0.0)

    cqn = (_rms(proj[:, _C_CQ:_C_CQ + Q_LORA]) * gcq_ref[...]).astype(BF16)
    q = jnp.dot(cqn, wuq_ref[...], preferred_element_type=F32)
    q_scale = LOG2_E / math.sqrt(MLA_QK)
    for hd in range(MLA_HEADS):
        sl = slice(hd * HEAD_PAD, (hd + 1) * HEAD_PAD)
        qh = q[:, sl]
        qh = qh * lax.rsqrt(_seg_mean_sq(qh, LANES, MLA_QK) + RMS_EPS) * qn_ref[...]
        q_out[:, sl] = (_rope(qh, cos_m, sin_m) * q_scale).astype(BF16)

    ckvn = (_rms(proj[:, _C_CKV:_C_CKV + KV_LORA]) * gckv_ref[...]).astype(BF16)
    k_nope = jnp.dot(ckvn, wuk_ref[...], preferred_element_type=F32)
    _store_values_t(jnp.dot(ckvn, wuv_ref[...], preferred_element_type=F32), v_out, MLA_HEADS)
    kr = proj[:, _C_KR:_C_KR + LANES]
    for hd in range(MLA_HEADS):
        sl = slice(hd * HEAD_PAD, (hd + 1) * HEAD_PAD)
        kh = k_nope[:, sl] + kr
        kh = kh * lax.rsqrt(_seg_mean_sq(kh, LANES, MLA_QK) + RMS_EPS) * kn_ref[...]
        k_out[:, sl] = _rope(kh, cos_m, sin_m).astype(BF16)

    d_scale = LOG2_E / math.sqrt(DIFF_D)
    for c in range(2 * DIFF_HEADS * DIFF_D // LANES):
        sl = slice(c * LANES, (c + 1) * LANES)
        xq = proj[:, _C_DQ + c * LANES:_C_DQ + (c + 1) * LANES]
        xq = xq * lax.rsqrt(_seg_mean_sq(xq, DIFF_D, DIFF_D) + RMS_EPS) * dqn_ref[:, sl]
        dq_out[:, sl] = (_rope(xq, cos, sin) * d_scale).astype(BF16)
        xk = proj[:, _C_DK + c * LANES:_C_DK + (c + 1) * LANES]
        xk = xk * lax.rsqrt(_seg_mean_sq(xk, DIFF_D, DIFF_D) + RMS_EPS) * dkn_ref[:, sl]
        dk_out[:, sl] = _rope(xk, cos, sin).astype(BF16)
    _store_values_t(proj[:, _C_DV:_C_DV + DIFF_HEADS * LANES], dv_out, DIFF_HEADS)

    z = proj[:, _C_Z:_C_END]
    gate = z * (1.0 / (1.0 + jnp.exp(-z)))
    gate_out[...] = gate[:, :MLA_W + DIFF_W].astype(BF16)

    m_scale = 1.0 / math.sqrt(MEM_D)
    mq_parts = []
    for c in range(MEM_W // LANES):
        xm = proj[:, _C_MQ + c * LANES:_C_MQ + (c + 1) * LANES]
        xm = xm * lax.rsqrt(_seg_mean_sq(xm, MEM_D, MEM_D) + RMS_EPS) * mqn_ref[:, c * LANES:(c + 1) * LANES]
        mq_parts.append(xm * m_scale)
    mq = jnp.concatenate(mq_parts, axis=1)
    kmem = kmem_ref[0]
    vmem = vmem_ref[0]
    lane_w = _lane_iota((1, MEM_W))
    o = jnp.zeros(mq.shape, F32)
    for hd in range(MEM_HEADS):
        head_mask = (lane_w // MEM_D) == hd
        qh = jnp.where(head_mask, mq, 0.0).astype(BF16)
        s = lax.dot_general(qh, kmem, _NT, preferred_element_type=F32)
        p = jnp.exp(s - jnp.max(s, axis=-1, keepdims=True))
        inv_l = 1.0 / jnp.sum(p, axis=-1, keepdims=True)
        vh = jnp.where(head_mask, vmem, jnp.zeros_like(vmem))
        o = o + jnp.dot(p.astype(BF16), vh, preferred_element_type=F32) * inv_l
    ymem_out[...] = (o * gate[:, MLA_W + DIFF_W:]).astype(BF16)


def _proj_call(x2d, cos_t, sin_t, kmem, vmem, p, seq):
    n = x2d.shape[0]
    tm = TM_PROJ
    per_batch = seq // tm
    row = lambda i: (i, 0)
    const = lambda i: (0, 0)
    full = lambda a: pl.BlockSpec(a.shape, const)
    mem_spec = pl.BlockSpec((1,) + kmem.shape[1:], lambda i: (i // per_batch, 0, 0))
    in_specs = [pl.BlockSpec((tm, D_MODEL), row), pl.BlockSpec((tm, LANES), row), pl.BlockSpec((tm, LANES), row),
                full(p["gx"]), full(p["w_in"]), full(p["gcq"]), full(p["gckv"]),
                full(p["w_uq"]), full(p["w_uk"]), full(p["w_uv"]),
                full(p["qn"]), full(p["kn"]), full(p["dqn"]), full(p["dkn"]), full(p["mqn"]),
                mem_spec, mem_spec]

    def rows_out(width):
        return pl.BlockSpec((tm, width), row), jax.ShapeDtypeStruct((n, width), BF16)

    def values_t_out(heads):
        return (pl.BlockSpec((1, heads * VT_ROWS, tm), lambda i: (i, 0, 0)),
                jax.ShapeDtypeStruct((n // tm, heads * VT_ROWS, tm), BF16))

    outs = [rows_out(MLA_HEADS * HEAD_PAD), rows_out(MLA_HEADS * HEAD_PAD), values_t_out(MLA_HEADS),
            rows_out(2 * DIFF_HEADS * DIFF_D), rows_out(2 * DIFF_HEADS * DIFF_D), values_t_out(DIFF_HEADS),
            rows_out(MEM_W), rows_out(MLA_W + DIFF_W)]
    return pl.pallas_call(
        _proj_kernel,
        grid=(n // tm,),
        in_specs=in_specs,
        out_specs=[o[0] for o in outs],
        out_shape=[o[1] for o in outs],
        compiler_params=pltpu.CompilerParams(dimension_semantics=("arbitrary",), vmem_limit_bytes=VMEM_LIMIT),
        name="proj_prep",
    )(x2d, cos_t, sin_t, p["gx"], p["w_in"], p["gcq"], p["gckv"], p["w_uq"], p["w_uk"], p["w_uv"],
      p["qn"], p["kn"], p["dqn"], p["dkn"], p["mqn"], kmem, vmem)


def _causal_keep_t(tk, tq, key_offset):
    r = lax.broadcasted_iota(jnp.int32, (tk, tq), 0)
    c = lax.broadcasted_iota(jnp.int32, (tk, tq), 1)
    return (r + key_offset) <= c


def _score_block(k_blk, q_blk, slot, idx):
    s_ref, mb_ref = slot
    s = lax.dot_general(k_blk, q_blk, _NT, preferred_element_type=F32)
    s_ref[idx] = s
    mb_ref[idx] = jnp.max(s, axis=0, keepdims=True)


def _softmax_pv_block(slot, idx, vt_blk, keep, m_ref, acc_ref):
    s_ref, mb_ref = slot
    s = s_ref[idx]
    if keep is None:
        mb = mb_ref[idx]
    else:
        s = jnp.where(keep, s, NEG_INF)
        mb = jnp.max(s, axis=0, keepdims=True)
    m_old = m_ref[idx]
    m_new = jnp.maximum(m_old, mb)
    alpha = jnp.exp2(m_old - m_new)
    p = jnp.exp2(s - m_new).astype(BF16)
    acc_ref[idx] = acc_ref[idx] * alpha + jnp.dot(vt_blk, p, preferred_element_type=F32)
    m_ref[idx] = m_new


def _causal_sweep(qi, tq, tk, slot_a, slot_b, scores, update):
    keep = _causal_keep_t(tk, tq, 0)
    scores(0, slot_a)

    def body(i, carry):
        scores(2 * i + 1, slot_b)
        update(2 * i, slot_a, None)
        scores(2 * i + 2, slot_a)
        update(2 * i + 1, slot_b, None)
        return carry

    lax.fori_loop(0, qi // 2, body, 0)

    @pl.when(qi % 2 == 0)
    def _():
        update(qi, slot_a, keep)

    @pl.when(qi % 2 == 1)
    def _():
        scores(qi, slot_b)
        update(qi - 1, slot_a, None)
        update(qi, slot_b, keep)


def _normalised_t(acc):
    return acc[:V_DIM] * (1.0 / acc[V_DIM:V_DIM + 1])


def _mla_attn_kernel(q_ref, k_ref, vt_ref, gate_ref, o_ref, sa_ref, mba_ref, sb_ref, mbb_ref, m_ref, acc_ref,
                     *, tq, tk):
    qi = pl.program_id(2)
    m_ref[...] = jnp.full(m_ref.shape, NEG_INF, F32)
    acc_ref[...] = jnp.zeros(acc_ref.shape, F32)

    def scores(j, slot):
        rows = pl.ds(pl.multiple_of(j * tk, tk), tk)
        for hd in range(2):
            sl = slice(hd * HEAD_PAD, (hd + 1) * HEAD_PAD)
            _score_block(k_ref[rows, sl], q_ref[:, sl], slot, hd)

    def update(j, slot, keep):
        for hd in range(2):
            _softmax_pv_block(slot, hd, vt_ref[j, hd * VT_ROWS:(hd + 1) * VT_ROWS, :], keep, m_ref, acc_ref)

    _causal_sweep(qi, tq, tk, (sa_ref, mba_ref), (sb_ref, mbb_ref), scores, update)
    o_t = jnp.concatenate([_normalised_t(acc_ref[0]), _normalised_t(acc_ref[1])], axis=0)
    o_ref[...] = (o_t.T * gate_ref[...].astype(F32)).astype(BF16)


def _mla_attn_call(q, k, vt, gate, batch, seq):
    tq, tk = TQ_MLA, TK
    nq = seq // tq
    pairs = MLA_HEADS // 2
    qmap = lambda b, hp, qi: (b * nq + qi, hp)
    return pl.pallas_call(
        functools.partial(_mla_attn_kernel, tq=tq, tk=tk),
        grid=(batch, pairs, nq),
        in_specs=[pl.BlockSpec((tq, 2 * HEAD_PAD), qmap),
                  pl.BlockSpec((seq, 2 * HEAD_PAD), lambda b, hp, qi: (b, hp)),
                  pl.BlockSpec((seq // tk, 2 * VT_ROWS, tk), lambda b, hp, qi: (b, hp, 0)),
                  pl.BlockSpec((tq, LANES), qmap)],
        out_specs=pl.BlockSpec((tq, LANES), qmap),
        out_shape=jax.ShapeDtypeStruct((batch * seq, MLA_W), BF16),
        scratch_shapes=[pltpu.VMEM((2, tk, tq), F32), pltpu.VMEM((2, 1, tq), F32),
                        pltpu.VMEM((2, tk, tq), F32), pltpu.VMEM((2, 1, tq), F32),
                        pltpu.VMEM((2, 1, tq), F32), pltpu.VMEM((2, VT_ROWS, tq), F32)],
        compiler_params=pltpu.CompilerParams(dimension_semantics=("arbitrary",) * 3, vmem_limit_bytes=VMEM_LIMIT),
        name="mla_attn",
    )(q, k, vt, gate)


def _diff_attn_kernel(q_ref, k_ref, vt_ref, gate_ref, lam_ref, sg_ref, o_ref, qm_ref, sa_ref, mba_ref, sb_ref, mbb_ref,
                      m_ref, acc_ref, *, tq, tk, lam_init):
    qi = pl.program_id(2)
    lane = _lane_iota((1, LANES))
    q = q_ref[...]
    for i in range(4):
        qm_ref[i] = jnp.where((lane // DIFF_D) == i, q, jnp.zeros_like(q))
    m_ref[...] = jnp.full(m_ref.shape, NEG_INF, F32)
    acc_ref[...] = jnp.zeros(acc_ref.shape, F32)

    def scores(j, slot):
        k_blk = k_ref[pl.ds(pl.multiple_of(j * tk, tk), tk), :]
        for idx in range(4):
            _score_block(k_blk, qm_ref[idx], slot, idx)

    def update(j, slot, keep):
        for idx in range(4):
            hd = idx // 2
            _softmax_pv_block(slot, idx, vt_ref[j, hd * VT_ROWS:(hd + 1) * VT_ROWS, :], keep, m_ref, acc_ref)

    _causal_sweep(qi, tq, tk, (sa_ref, mba_ref), (sb_ref, mbb_ref), scores, update)

    lv = lam_ref[...]
    lam = (jnp.exp(jnp.sum(lv[0:1] * lv[1:2], axis=-1, keepdims=True))
           - jnp.exp(jnp.sum(lv[2:3] * lv[3:4], axis=-1, keepdims=True)) + lam_init)
    heads_t = []
    for hd in range(2):
        o = _normalised_t(acc_ref[2 * hd]) - lam * _normalised_t(acc_ref[2 * hd + 1])
        heads_t.append(o * lax.rsqrt(jnp.mean(o * o, axis=0, keepdims=True) + RMS_EPS))
    o = jnp.concatenate(heads_t, axis=0).T * sg_ref[...] * (1.0 - lam_init)
    o_ref[...] = (o * gate_ref[...].astype(F32)).astype(BF16)


def _diff_attn_call(q, k, vt, gate, lam_pad, subln_t, batch, seq, lam_init):
    tq, tk = TQ_DIFF, TK
    nq = seq // tq
    pairs = DIFF_HEADS // 2
    gate_off = MLA_W // LANES
    qmap = lambda b, hp, qi: (b * nq + qi, hp)
    const = lambda b, hp, qi: (0, 0)
    return pl.pallas_call(
        functools.partial(_diff_attn_kernel, tq=tq, tk=tk, lam_init=lam_init),
        grid=(batch, pairs, nq),
        in_specs=[pl.BlockSpec((tq, LANES), qmap),
                  pl.BlockSpec((seq, LANES), lambda b, hp, qi: (b, hp)),
                  pl.BlockSpec((seq // tk, 2 * VT_ROWS, tk), lambda b, hp, qi: (b, hp, 0)),
                  pl.BlockSpec((tq, LANES), lambda b, hp, qi: (b * nq + qi, gate_off + hp)),
                  pl.BlockSpec(lam_pad.shape, const),
                  pl.BlockSpec(subln_t.shape, const)],
        out_specs=pl.BlockSpec((tq, LANES), qmap),
        out_shape=jax.ShapeDtypeStruct((batch * seq, DIFF_W), BF16),
        scratch_shapes=[pltpu.VMEM((4, tq, LANES), BF16),
                        pltpu.VMEM((4, tk, tq), F32), pltpu.VMEM((4, 1, tq), F32),
                        pltpu.VMEM((4, tk, tq), F32), pltpu.VMEM((4, 1, tq), F32),
                        pltpu.VMEM((4, 1, tq), F32), pltpu.VMEM((4, VT_ROWS, tq), F32)],
        compiler_params=pltpu.CompilerParams(dimension_semantics=("arbitrary",) * 3, vmem_limit_bytes=VMEM_LIMIT),
        name="diff_attn",
    )(q, k, vt, gate, lam_pad, subln_t)


def _out_kernel(x_ref, ymla_ref, ydiff_ref, ymem_ref, w_ref, o_ref):
    y = jnp.concatenate([ymla_ref[...], ydiff_ref[...], ymem_ref[...]], axis=1)
    o_ref[...] = x_ref[...] + jnp.dot(y, w_ref[...], preferred_element_type=F32)


def _out_call(x2d, y_mla, y_diff, y_mem, w_out):
    n = x2d.shape[0]
    tm = TM_OUT
    row = lambda i: (i, 0)
    return pl.pallas_call(
        _out_kernel,
        grid=(n // tm,),
        in_specs=[pl.BlockSpec((tm, D_MODEL), row), pl.BlockSpec((tm, MLA_W), row),
                  pl.BlockSpec((tm, DIFF_W), row), pl.BlockSpec((tm, MEM_W), row),
                  pl.BlockSpec(w_out.shape, lambda i: (0, 0))],
        out_specs=pl.BlockSpec((tm, D_MODEL), row),
        out_shape=jax.ShapeDtypeStruct((n, D_MODEL), F32),
        compiler_params=pltpu.CompilerParams(dimension_semantics=("arbitrary",), vmem_limit_bytes=VMEM_LIMIT),
        name="out_proj",
    )(x2d, y_mla, y_diff, y_mem, w_out)


def _pad_cols(a, width):
    return jnp.pad(a, ((0, 0), (0, width - a.shape[1])))


def _head_padded(a, heads, width):
    rows = a.shape[0]
    a3 = a.reshape(rows, heads, width)
    return jnp.pad(a3, ((0, 0), (0, 0), (0, LANES - width))).reshape(rows, heads * LANES)


def _layer_params(l, norm_g, w_in, mla_q_norm_g, mla_kv_norm_g, w_uq, w_ukv, mla_qn_g, mla_kn_g,
                  diff_qn_g, diff_kn_g, mem_qn_g):
    w = w_in[l]
    o_cq, o_ckv, o_kr = 0, Q_LORA, Q_LORA + KV_LORA
    o_dq = o_kr + MLA_ROPE
    o_dk = o_dq + 2 * DIFF_HEADS * DIFF_D
    o_dv = o_dk + 2 * DIFF_HEADS * DIFF_D
    o_mq = o_dv + DIFF_W
    o_z = o_mq + MEM_W
    zeros = lambda c: jnp.zeros((w.shape[0], c), w.dtype)
    w_in_p = jnp.concatenate([
        w[:, o_cq:o_ckv], w[:, o_ckv:o_kr],
        zeros(MLA_NOPE), w[:, o_kr:o_dq], zeros(LANES - MLA_QK),
        w[:, o_dq:o_dk], w[:, o_dk:o_dv],
        _head_padded(w[:, o_dv:o_mq], DIFF_HEADS, DIFF_V),
        w[:, o_mq:o_z], w[:, o_z:]], axis=1).astype(BF16)
    w_uq_p = jnp.pad(w_uq[l].reshape(Q_LORA, MLA_HEADS, MLA_QK),
                     ((0, 0), (0, 0), (0, HEAD_PAD - MLA_QK))).reshape(Q_LORA, MLA_HEADS * HEAD_PAD).astype(BF16)
    kv3 = w_ukv[l].reshape(KV_LORA, MLA_HEADS, MLA_NOPE + MLA_V)
    w_uk_p = jnp.pad(kv3[:, :, :MLA_NOPE],
                     ((0, 0), (0, 0), (0, HEAD_PAD - MLA_NOPE))).reshape(KV_LORA, MLA_HEADS * HEAD_PAD).astype(BF16)
    w_uv_p = _head_padded(kv3[:, :, MLA_NOPE:].reshape(KV_LORA, MLA_HEADS * MLA_V), MLA_HEADS, MLA_V).astype(BF16)
    return {
        "gx": norm_g[l][None, :],
        "w_in": w_in_p,
        "gcq": mla_q_norm_g[l][None, :],
        "gckv": mla_kv_norm_g[l][None, :],
        "w_uq": w_uq_p, "w_uk": w_uk_p, "w_uv": w_uv_p,
        "qn": _pad_cols(mla_qn_g[l][None, :], HEAD_PAD),
        "kn": _pad_cols(mla_kn_g[l][None, :], HEAD_PAD),
        "dqn": jnp.tile(diff_qn_g[l], 2 * DIFF_HEADS)[None, :],
        "dkn": jnp.tile(diff_kn_g[l], 2 * DIFF_HEADS)[None, :],
        "mqn": jnp.tile(mem_qn_g[l], MEM_HEADS)[None, :],
    }


def kernel(x, mem, positions, norm_g, w_in, mla_q_norm_g, mla_kv_norm_g, w_uq, w_ukv, mla_qn_g, mla_kn_g,
           diff_qn_g, diff_kn_g, diff_lambda, diff_subln_g, mem_norm_g, w_mem_kv, mem_qn_g, mem_kn_g, w_out):
    batch, seq, d_model = x.shape
    depth = w_in.shape[0]
    assert d_model == D_MODEL and MLA_V == DIFF_V and TQ_MLA == TK and TQ_DIFF == TK
    assert seq % max(TM_PROJ, TQ_MLA, TQ_DIFF, TM_OUT) == 0 and (batch * seq) % 1024 == 0
    cos_t, sin_t = _rope_tables(positions)
    kmem_all, vmem_all = _mem_kv(mem, mem_norm_g, w_mem_kv, mem_kn_g)
    x2d = x.reshape(batch * seq, D_MODEL)
    for l in range(depth):
        lam_init = 0.8 - 0.6 * math.exp(-0.3 * l)
        p = _layer_params(l, norm_g, w_in, mla_q_norm_g, mla_kv_norm_g, w_uq, w_ukv, mla_qn_g, mla_kn_g,
                          diff_qn_g, diff_kn_g, mem_qn_g)
        q, k, v, dq, dk, dv, y_mem, gate = _proj_call(x2d, cos_t, sin_t, kmem_all[l], vmem_all[l], p, seq)
        y_mla = _mla_attn_call(q, k, v, gate, batch, seq)
        lam_pad = jnp.pad(diff_lambda[l], ((0, 4), (0, LANES - DIFF_D)))
        subln_t = jnp.tile(diff_subln_g[l], 2)[None, :]
        y_diff = _diff_attn_call(dq, dk, dv, gate, lam_pad, subln_t, batch, seq, lam_init)
        x2d = _out_call(x2d, y_mla, y_diff, y_mem, w_out[l].astype(BF16))
    return x2d.reshape(batch, seq, D_MODEL)
```

```python
import functools
import math

import numpy as np
import jax
import jax.numpy as jnp
from jax import lax
from jax.experimental import pallas as pl
from jax.experimental.pallas import tpu as pltpu

F32 = jnp.float32
BF16 = jnp.bfloat16

D_MODEL = 1024
ROPE_THETA = 10000.0
RMS_EPS = 1e-6
NEG_INF = -1e30
LOG2_E = math.log2(math.e)

MLA_HEADS = 8
Q_LORA = 256
KV_LORA = 128
MLA_NOPE = 64
MLA_ROPE = 32
MLA_QK = MLA_NOPE + MLA_ROPE
MLA_V = 64
MLA_W = MLA_HEADS * MLA_V

DIFF_HEADS = 4
DIFF_D = 32
DIFF_V = 2 * DIFF_D
DIFF_W = DIFF_HEADS * DIFF_V
DIFF_QK_W = 2 * DIFF_HEADS * DIFF_D

MEM_HEADS = 4
MEM_D = 64
MEM_W = MEM_HEADS * MEM_D

D_MIX = MLA_W + DIFF_W + MEM_W

LANES = 128
MXU_DIM = 256
ROPE_D = 32
HEAD_PAD = LANES

_C_CQ = 0
_C_CKV = _C_CQ + Q_LORA
_C_KR = _C_CKV + KV_LORA
_C_KR_SW = _C_KR + LANES
_C_DQ = _C_KR_SW + LANES
_C_DQ_SW = _C_DQ + DIFF_QK_W
_C_DK = _C_DQ_SW + DIFF_QK_W
_C_DK_SW = _C_DK + DIFF_QK_W
_C_DV = _C_DK_SW + DIFF_QK_W
_C_MQ = _C_DV + DIFF_HEADS * LANES
_C_Z = _C_MQ + MEM_W
_C_END = _C_Z + D_MIX

VMEM_LIMIT = 56 * 1024 * 1024

V_DIM = MLA_V
VT_ROWS = 80

TK = 512
TM_PROJ = TK
TM_OUT = 512
TQ_MLA = 512
TQ_DIFF = 512
SCORE_SLOTS = 3

_NT = (((1,), (1,)), ((), ()))


def _rms(x):
    return x * lax.rsqrt(jnp.mean(x * x, axis=-1, keepdims=True) + RMS_EPS)


def _lane_iota(shape):
    return lax.broadcasted_iota(jnp.int32, shape, len(shape) - 1)


def _seg_sum_sq(x, bd_ref):
    sq = x * x
    hi = sq.astype(BF16)
    lo = (sq - hi.astype(F32)).astype(BF16)
    bd = bd_ref[...]
    return jnp.dot(hi, bd, preferred_element_type=F32) + jnp.dot(lo, bd, preferred_element_type=F32)


def _rope_table_kernel(pos_ref, inv_ref, cos_ref, sin_ref):
    ang = pos_ref[...].astype(F32) * inv_ref[...]
    lane = _lane_iota(ang.shape)
    sign = jnp.where((lane & (ROPE_D // 2)) == 0, -1.0, 1.0)
    cos_ref[...] = jnp.cos(ang)
    sin_ref[...] = jnp.sin(ang) * sign


def _rope_tables(positions):
    n = positions.size
    tm = 1024
    inv = ROPE_THETA ** (-jnp.arange(0, ROPE_D, 2, dtype=F32) / ROPE_D)
    inv_t = jnp.tile(inv, LANES // (ROPE_D // 2))[None, :]
    pos = positions.reshape(n, 1)
    return pl.pallas_call(
        _rope_table_kernel,
        grid=(n // tm,),
        in_specs=[pl.BlockSpec((tm, 1), lambda i: (i, 0)),
                  pl.BlockSpec((1, LANES), lambda i: (0, 0))],
        out_specs=[pl.BlockSpec((tm, LANES), lambda i: (i, 0)),
                   pl.BlockSpec((tm, LANES), lambda i: (i, 0))],
        out_shape=[jax.ShapeDtypeStruct((n, LANES), F32)] * 2,
        name="rope_tables",
    )(pos, inv_t)


def _mem_kv_kernel(mem_ref, g_ref, w_ref, kn_ref, bd_ref, k_ref, v_ref):
    m = mem_ref[0]
    mn = (_rms(m) * g_ref[0]).astype(BF16)
    kv = jnp.dot(mn, w_ref[0], preferred_element_type=F32)
    k = kv[:, :MEM_W]
    k = k * lax.rsqrt(_seg_sum_sq(k, bd_ref) * (1.0 / MEM_D) + RMS_EPS) * kn_ref[0]
    k_ref[0, 0] = k.astype(BF16)
    v_ref[0, 0] = kv[:, MEM_W:].astype(BF16)


def _mem_kv(mem, mem_norm_g, w_mem_kv, mem_kn_g, bd_mem):
    depth = w_mem_kv.shape[0]
    b, m, d = mem.shape
    kn_t = jnp.tile(mem_kn_g, (1, MEM_HEADS))[:, None, :]
    return pl.pallas_call(
        _mem_kv_kernel,
        grid=(depth, b),
        in_specs=[pl.BlockSpec((1, m, d), lambda l, i: (i, 0, 0)),
                  pl.BlockSpec((1, 1, d), lambda l, i: (l, 0, 0)),
                  pl.BlockSpec((1, d, 2 * MEM_W), lambda l, i: (l, 0, 0)),
                  pl.BlockSpec((1, 1, MEM_W), lambda l, i: (l, 0, 0)),
                  pl.BlockSpec(bd_mem.shape, lambda l, i: (0, 0))],
        out_specs=[pl.BlockSpec((1, 1, m, MEM_W), lambda l, i: (l, i, 0, 0)),
                   pl.BlockSpec((1, 1, m, MEM_W), lambda l, i: (l, i, 0, 0))],
        out_shape=[jax.ShapeDtypeStruct((depth, b, m, MEM_W), BF16)] * 2,
        name="mem_kv",
    )(mem, mem_norm_g[:, None, :], w_mem_kv.astype(BF16), kn_t, bd_mem)


_PROJ_PARAMS = ("gx", "w_in", "gcq", "gckv", "w_uq", "w_uq_sw", "w_uk", "w_uv_t",
                "q_gc1", "q_gc0", "q_gs", "k_gc1", "k_gc0", "k_gs",
                "dq_gc", "dq_gs", "dk_gc", "dk_gs", "mqn", "bd_diff", "bd_mem")


def _proj_kernel(x_ref, cos_ref, sin_ref, gx_ref, win_ref, gcq_ref, gckv_ref, wuq_ref, wuqsw_ref, wuk_ref, wuvt_ref,
                 qgc1_ref, qgc0_ref, qgs_ref, kgc1_ref, kgc0_ref, kgs_ref,
                 dqgc_ref, dqgs_ref, dkgc_ref, dkgs_ref, mqn_ref, bdd_ref, bdm_ref, kmem_ref, vmem_ref,
                 q_out, k_out, v_out, dq_out, dk_out, dv_out, ymem_out, gate_out):
    tm = x_ref.shape[0]
    h = (_rms(x_ref[...]) * gx_ref[...]).astype(BF16)
    proj = jnp.dot(h, win_ref[...], preferred_element_type=F32)
    cos = cos_ref[...]
    sin = sin_ref[...]

    cqn = (_rms(proj[:, _C_CQ:_C_CQ + Q_LORA]) * gcq_ref[...]).astype(BF16)
    q = jnp.dot(cqn, wuq_ref[...], preferred_element_type=F32)
    q_sw = jnp.dot(cqn, wuqsw_ref[...], preferred_element_type=F32)
    q_gc = cos * qgc1_ref[...] + qgc0_ref[...]
    q_gs = sin * qgs_ref[...]
    for hd in range(MLA_HEADS):
        sl = slice(hd * HEAD_PAD, (hd + 1) * HEAD_PAD)
        qh = q[:, sl]
        r = lax.rsqrt(jnp.sum(qh * qh, axis=-1, keepdims=True) * (1.0 / MLA_QK) + RMS_EPS)
        q_out[:, sl] = ((qh * q_gc + q_sw[:, sl] * q_gs) * r).astype(BF16)

    ckv = _rms(proj[:, _C_CKV:_C_CKV + KV_LORA]) * gckv_ref[...]
    ckvn = ckv.astype(BF16)
    k_nope = jnp.dot(ckvn, wuk_ref[...], preferred_element_type=F32)
    kr = proj[:, _C_KR:_C_KR + LANES]
    k_gc = cos * kgc1_ref[...] + kgc0_ref[...]
    kr_rot = proj[:, _C_KR_SW:_C_KR_SW + LANES] * (sin * kgs_ref[...])
    for hd in range(MLA_HEADS):
        sl = slice(hd * HEAD_PAD, (hd + 1) * HEAD_PAD)
        kh = k_nope[:, sl] + kr
        r = lax.rsqrt(jnp.sum(kh * kh, axis=-1, keepdims=True) * (1.0 / MLA_QK) + RMS_EPS)
        k_out[:, sl] = ((kh * k_gc + kr_rot) * r).astype(BF16)
    vt = jnp.dot(wuvt_ref[...], ckv.T.astype(BF16), preferred_element_type=F32)
    ones_row = (lax.broadcasted_iota(jnp.int32, (VT_ROWS, tm), 0) == V_DIM).astype(F32)
    for hd in range(MLA_HEADS):
        rows = slice(hd * VT_ROWS, (hd + 1) * VT_ROWS)
        v_out[0, rows, :] = (vt[rows] + ones_row).astype(BF16)

    for raw_c, sw_c, gc_ref, gs_ref, out in ((_C_DQ, _C_DQ_SW, dqgc_ref, dqgs_ref, dq_out),
                                             (_C_DK, _C_DK_SW, dkgc_ref, dkgs_ref, dk_out)):
        raw = proj[:, raw_c:raw_c + DIFF_QK_W]
        r = lax.rsqrt(_seg_sum_sq(raw, bdd_ref) * (1.0 / DIFF_D) + RMS_EPS)
        for c in range(DIFF_QK_W // LANES):
            sl = slice(c * LANES, (c + 1) * LANES)
            rot = raw[:, sl] * (cos * gc_ref[:, sl]) + proj[:, sw_c + c * LANES:sw_c + (c + 1) * LANES] * (sin * gs_ref[:, sl])
            out[:, sl] = (rot * r[:, sl]).astype(BF16)
    ones_lane = (_lane_iota((1, LANES)) == V_DIM).astype(F32)
    for hd in range(DIFF_HEADS):
        dv_t = (proj[:, _C_DV + hd * LANES:_C_DV + (hd + 1) * LANES] + ones_lane).T
        dv_out[0, hd * VT_ROWS:(hd + 1) * VT_ROWS, :] = dv_t[:VT_ROWS].astype(BF16)

    z = proj[:, _C_Z:_C_END]
    gate = z * (1.0 / (1.0 + jnp.exp(-z)))
    gate_out[...] = gate[:, :MLA_W + DIFF_W].astype(BF16)

    mq = proj[:, _C_MQ:_C_MQ + MEM_W]
    mq = mq * lax.rsqrt(_seg_sum_sq(mq, bdm_ref) * (1.0 / MEM_D) + RMS_EPS) * mqn_ref[...]
    kmem = kmem_ref[0]
    vmem = vmem_ref[0]
    lane_w = _lane_iota((1, MEM_W))
    o = jnp.zeros(mq.shape, F32)
    for hd in range(MEM_HEADS):
        head_mask = (lane_w // MEM_D) == hd
        qh = jnp.where(head_mask, mq, 0.0).astype(BF16)
        s = lax.dot_general(qh, kmem, _NT, preferred_element_type=F32)
        p = jnp.exp(s - jnp.max(s, axis=-1, keepdims=True))
        inv_l = 1.0 / jnp.sum(p, axis=-1, keepdims=True)
        vh = jnp.where(head_mask, vmem, jnp.zeros_like(vmem))
        o = o + jnp.dot(p.astype(BF16), vh, preferred_element_type=F32) * inv_l
    ymem_out[...] = (o * gate[:, MLA_W + DIFF_W:]).astype(BF16)


def _proj_call(x2d, cos_t, sin_t, kmem, vmem, p, seq):
    n = x2d.shape[0]
    tm = TM_PROJ
    per_batch = seq // tm
    row = lambda i: (i, 0)
    const = lambda i: (0, 0)
    params = [p[name] for name in _PROJ_PARAMS]
    mem_spec = pl.BlockSpec((1,) + kmem.shape[1:], lambda i: (i // per_batch, 0, 0))
    in_specs = ([pl.BlockSpec((tm, D_MODEL), row), pl.BlockSpec((tm, LANES), row), pl.BlockSpec((tm, LANES), row)]
                + [pl.BlockSpec(a.shape, const) for a in params] + [mem_spec, mem_spec])

    def rows_out(width):
        return pl.BlockSpec((tm, width), row), jax.ShapeDtypeStruct((n, width), BF16)

    def values_t_out(heads):
        return (pl.BlockSpec((1, heads * VT_ROWS, tm), lambda i: (i, 0, 0)),
                jax.ShapeDtypeStruct((n // tm, heads * VT_ROWS, tm), BF16))

    outs = [rows_out(MLA_HEADS * HEAD_PAD), rows_out(MLA_HEADS * HEAD_PAD), values_t_out(MLA_HEADS),
            rows_out(DIFF_QK_W), rows_out(DIFF_QK_W), values_t_out(DIFF_HEADS),
            rows_out(MEM_W), rows_out(MLA_W + DIFF_W)]
    return pl.pallas_call(
        _proj_kernel,
        grid=(n // tm,),
        in_specs=in_specs,
        out_specs=[o[0] for o in outs],
        out_shape=[o[1] for o in outs],
        compiler_params=pltpu.CompilerParams(dimension_semantics=("arbitrary",), vmem_limit_bytes=VMEM_LIMIT),
        name="proj_prep",
    )(x2d, cos_t, sin_t, *params, kmem, vmem)


def _causal_keep_t(tk, tq, key_offset):
    r = lax.broadcasted_iota(jnp.int32, (tk, tq), 0)
    c = lax.broadcasted_iota(jnp.int32, (tk, tq), 1)
    return (r + key_offset) <= c


def _score_block(k_blk, q_blk, slot, idx):
    s_ref, mb_ref = slot
    s = lax.dot_general(k_blk, q_blk, _NT, preferred_element_type=F32)
    s_ref[idx] = s
    mb_ref[idx] = jnp.max(s, axis=0, keepdims=True)


def _softmax_pv_block(slot, idx, vt_blk, keep, m_ref, acc_ref):
    s_ref, mb_ref = slot
    s = s_ref[idx]
    if keep is None:
        mb = mb_ref[idx]
    else:
        s = jnp.where(keep, s, NEG_INF)
        mb = jnp.max(s, axis=0, keepdims=True)
    m_old = m_ref[idx]
    m_new = jnp.maximum(m_old, mb)
    alpha = jnp.exp2(m_old - m_new)
    p = jnp.exp2(s - m_new).astype(BF16)
    acc_ref[idx] = acc_ref[idx] * alpha + jnp.dot(vt_blk, p, preferred_element_type=F32)
    m_ref[idx] = m_new


def _causal_sweep(qi, tq, tk, slots, scores, update):
    n = len(slots)
    keep = _causal_keep_t(tk, tq, 0)
    scores(0, slots[0])

    def body(i, carry):
        for u in range(n):
            scores(n * i + u + 1, slots[(u + 1) % n])
            update(n * i + u, slots[u], None)
        return carry

    full = qi // n
    lax.fori_loop(0, full, body, 0)
    first = full * n
    for rem in range(n):
        @pl.when(qi - first == rem)
        def _(rem=rem):
            for u in range(rem):
                scores(first + u + 1, slots[u + 1])
                update(first + u, slots[u], None)
            update(qi, slots[rem], keep)


def _normalised_t(acc):
    return acc[:V_DIM] * (1.0 / acc[V_DIM:V_DIM + 1])


def _score_scratch(count, tk, tq):
    shapes = []
    for _ in range(SCORE_SLOTS):
        shapes += [pltpu.VMEM((count, tk, tq), F32), pltpu.VMEM((count, 1, tq), F32)]
    return shapes


def _slots(refs):
    return [(refs[2 * i], refs[2 * i + 1]) for i in range(SCORE_SLOTS)]


def _mla_attn_kernel(q_ref, k_ref, vt_ref, gate_ref, o_ref, *scratch, tq, tk):
    m_ref, acc_ref = scratch[-2:]
    qi = pl.program_id(2)
    m_ref[...] = jnp.full(m_ref.shape, NEG_INF, F32)
    acc_ref[...] = jnp.zeros(acc_ref.shape, F32)

    def scores(j, slot):
        rows = pl.ds(pl.multiple_of(j * tk, tk), tk)
        for hd in range(2):
            sl = slice(hd * HEAD_PAD, (hd + 1) * HEAD_PAD)
            _score_block(k_ref[rows, sl], q_ref[:, sl], slot, hd)

    def update(j, slot, keep):
        for hd in range(2):
            _softmax_pv_block(slot, hd, vt_ref[j, hd * VT_ROWS:(hd + 1) * VT_ROWS, :], keep, m_ref, acc_ref)

    _causal_sweep(qi, tq, tk, _slots(scratch), scores, update)
    o_t = jnp.concatenate([_normalised_t(acc_ref[0]), _normalised_t(acc_ref[1])], axis=0)
    o_ref[...] = (o_t.T * gate_ref[...].astype(F32)).astype(BF16)


def _mla_attn_call(q, k, vt, gate, batch, seq):
    tq, tk = TQ_MLA, TK
    nq = seq // tq
    pairs = MLA_HEADS // 2
    qmap = lambda b, hp, qi: (b * nq + qi, hp)
    return pl.pallas_call(
        functools.partial(_mla_attn_kernel, tq=tq, tk=tk),
        grid=(batch, pairs, nq),
        in_specs=[pl.BlockSpec((tq, 2 * HEAD_PAD), qmap),
                  pl.BlockSpec((seq, 2 * HEAD_PAD), lambda b, hp, qi: (b, hp)),
                  pl.BlockSpec((seq // tk, 2 * VT_ROWS, tk), lambda b, hp, qi: (b, hp, 0)),
                  pl.BlockSpec((tq, LANES), qmap)],
        out_specs=pl.BlockSpec((tq, LANES), qmap),
        out_shape=jax.ShapeDtypeStruct((batch * seq, MLA_W), BF16),
        scratch_shapes=_score_scratch(2, tk, tq) + [pltpu.VMEM((2, 1, tq), F32), pltpu.VMEM((2, VT_ROWS, tq), F32)],
        compiler_params=pltpu.CompilerParams(dimension_semantics=("arbitrary",) * 3, vmem_limit_bytes=VMEM_LIMIT),
        name="mla_attn",
    )(q, k, vt, gate)


def _diff_attn_kernel(q_ref, k_ref, vt_ref, gate_ref, lam_ref, sg_ref, o_ref, qm_ref, *scratch, tq, tk, lam_init):
    m_ref, acc_ref = scratch[-2:]
    qi = pl.program_id(2)
    lane = _lane_iota((1, LANES))
    q = q_ref[...]
    for i in range(4):
        qm_ref[i] = jnp.where((lane // DIFF_D) == i, q, jnp.zeros_like(q))
    m_ref[...] = jnp.full(m_ref.shape, NEG_INF, F32)
    acc_ref[...] = jnp.zeros(acc_ref.shape, F32)

    def scores(j, slot):
        k_blk = k_ref[pl.ds(pl.multiple_of(j * tk, tk), tk), :]
        for idx in range(4):
            _score_block(k_blk, qm_ref[idx], slot, idx)

    def update(j, slot, keep):
        for idx in range(4):
            hd = idx // 2
            _softmax_pv_block(slot, idx, vt_ref[j, hd * VT_ROWS:(hd + 1) * VT_ROWS, :], keep, m_ref, acc_ref)

    _causal_sweep(qi, tq, tk, _slots(scratch), scores, update)

    lv = lam_ref[...]
    lam = (jnp.exp(jnp.sum(lv[0:1] * lv[1:2], axis=-1, keepdims=True))
           - jnp.exp(jnp.sum(lv[2:3] * lv[3:4], axis=-1, keepdims=True)) + lam_init)
    heads_t = []
    for hd in range(2):
        o = _normalised_t(acc_ref[2 * hd]) - lam * _normalised_t(acc_ref[2 * hd + 1])
        heads_t.append(o * lax.rsqrt(jnp.mean(o * o, axis=0, keepdims=True) + RMS_EPS))
    o = jnp.concatenate(heads_t, axis=0).T * sg_ref[...] * (1.0 - lam_init)
    o_ref[...] = (o * gate_ref[...].astype(F32)).astype(BF16)


def _diff_attn_call(q, k, vt, gate, lam_pad, subln_t, batch, seq, lam_init):
    tq, tk = TQ_DIFF, TK
    nq = seq // tq
    pairs = DIFF_HEADS // 2
    gate_off = MLA_W // LANES
    qmap = lambda b, hp, qi: (b * nq + qi, hp)
    const = lambda b, hp, qi: (0, 0)
    return pl.pallas_call(
        functools.partial(_diff_attn_kernel, tq=tq, tk=tk, lam_init=lam_init),
        grid=(batch, pairs, nq),
        in_specs=[pl.BlockSpec((tq, LANES), qmap),
                  pl.BlockSpec((seq, LANES), lambda b, hp, qi: (b, hp)),
                  pl.BlockSpec((seq // tk, 2 * VT_ROWS, tk), lambda b, hp, qi: (b, hp, 0)),
                  pl.BlockSpec((tq, LANES), lambda b, hp, qi: (b * nq + qi, gate_off + hp)),
                  pl.BlockSpec(lam_pad.shape, const),
                  pl.BlockSpec(subln_t.shape, const)],
        out_specs=pl.BlockSpec((tq, LANES), qmap),
        out_shape=jax.ShapeDtypeStruct((batch * seq, DIFF_W), BF16),
        scratch_shapes=([pltpu.VMEM((4, tq, LANES), BF16)] + _score_scratch(4, tk, tq)
                        + [pltpu.VMEM((4, 1, tq), F32), pltpu.VMEM((4, VT_ROWS, tq), F32)]),
        compiler_params=pltpu.CompilerParams(dimension_semantics=("arbitrary",) * 3, vmem_limit_bytes=VMEM_LIMIT),
        name="diff_attn",
    )(q, k, vt, gate, lam_pad, subln_t)


def _out_kernel(x_ref, ymla_ref, ydiff_ref, ymem_ref, w_ref, o_ref):
    y = jnp.concatenate([ymla_ref[...], ydiff_ref[...], ymem_ref[...]], axis=1)
    o_ref[...] = x_ref[...] + jnp.dot(y, w_ref[...], preferred_element_type=F32)


def _out_call(x2d, y_mla, y_diff, y_mem, w_out):
    n = x2d.shape[0]
    tm = TM_OUT
    row = lambda i: (i, 0)
    return pl.pallas_call(
        _out_kernel,
        grid=(n // tm,),
        in_specs=[pl.BlockSpec((tm, D_MODEL), row), pl.BlockSpec((tm, MLA_W), row),
                  pl.BlockSpec((tm, DIFF_W), row), pl.BlockSpec((tm, MEM_W), row),
                  pl.BlockSpec(w_out.shape, lambda i: (0, 0))],
        out_specs=pl.BlockSpec((tm, D_MODEL), row),
        out_shape=jax.ShapeDtypeStruct((n, D_MODEL), F32),
        compiler_params=pltpu.CompilerParams(dimension_semantics=("arbitrary",), vmem_limit_bytes=VMEM_LIMIT),
        name="out_proj",
    )(x2d, y_mla, y_diff, y_mem, w_out)


def _block_diag_ones(seg):
    idx = np.arange(MXU_DIM) // seg
    return jnp.asarray(idx[:, None] == idx[None, :], dtype=BF16)


def _pair_swapped(a):
    lead = a.shape[:-1]
    a4 = a.reshape(lead + (a.shape[-1] // ROPE_D, 2, ROPE_D // 2))
    return a4[..., ::-1, :].reshape(a.shape)


def _head_padded(a, heads, width, offset=0):
    rows = a.shape[0]
    a3 = a.reshape(rows, heads, width)
    return jnp.pad(a3, ((0, 0), (0, 0), (offset, LANES - width - offset))).reshape(rows, heads * LANES)


def _rope_lane_vec(g_rope):
    return jnp.pad(g_rope, (MLA_NOPE, LANES - MLA_QK))[None, :]


def _layer_params(l, norm_g, w_in, mla_q_norm_g, mla_kv_norm_g, w_uq, w_ukv, mla_qn_g, mla_kn_g,
                  diff_qn_g, diff_kn_g, mem_qn_g):
    w = w_in[l]
    o_cq, o_ckv, o_kr = 0, Q_LORA, Q_LORA + KV_LORA
    o_dq = o_kr + MLA_ROPE
    o_dk = o_dq + DIFF_QK_W
    o_dv = o_dk + DIFF_QK_W
    o_mq = o_dv + DIFF_W
    o_z = o_mq + MEM_W
    w_kr, w_dq, w_dk = w[:, o_kr:o_dq], w[:, o_dq:o_dk], w[:, o_dk:o_dv]
    w_in_p = jnp.concatenate([
        w[:, o_cq:o_ckv], w[:, o_ckv:o_kr],
        _head_padded(w_kr, 1, MLA_ROPE, MLA_NOPE), _head_padded(_pair_swapped(w_kr), 1, MLA_ROPE, MLA_NOPE),
        w_dq, _pair_swapped(w_dq), w_dk, _pair_swapped(w_dk),
        _head_padded(w[:, o_dv:o_mq], DIFF_HEADS, DIFF_V),
        w[:, o_mq:o_z], w[:, o_z:]], axis=1).astype(BF16)

    uq3 = w_uq[l].reshape(Q_LORA, MLA_HEADS, MLA_QK)
    w_uq_p = _head_padded(w_uq[l], MLA_HEADS, MLA_QK).astype(BF16)
    w_uq_sw = _head_padded(_pair_swapped(uq3[:, :, MLA_NOPE:]).reshape(Q_LORA, MLA_HEADS * MLA_ROPE),
                           MLA_HEADS, MLA_ROPE, MLA_NOPE).astype(BF16)
    kv3 = w_ukv[l].reshape(KV_LORA, MLA_HEADS, MLA_NOPE + MLA_V)
    w_uk_p = _head_padded(kv3[:, :, :MLA_NOPE].reshape(KV_LORA, MLA_HEADS * MLA_NOPE), MLA_HEADS, MLA_NOPE).astype(BF16)
    w_uv_t = jnp.pad(jnp.transpose(kv3[:, :, MLA_NOPE:], (1, 2, 0)),
                     ((0, 0), (0, VT_ROWS - MLA_V), (0, 0))).reshape(MLA_HEADS * VT_ROWS, KV_LORA).astype(BF16)

    q_scale = LOG2_E / math.sqrt(MLA_QK)
    d_scale = LOG2_E / math.sqrt(DIFF_D)
    qn, kn = mla_qn_g[l], mla_kn_g[l]
    nope_vec = lambda g: jnp.pad(g[:MLA_NOPE], (0, LANES - MLA_NOPE))[None, :]
    dqn = jnp.tile(diff_qn_g[l], 2 * DIFF_HEADS)[None, :]
    dkn = jnp.tile(diff_kn_g[l], 2 * DIFF_HEADS)[None, :]
    return {
        "gx": norm_g[l][None, :],
        "w_in": w_in_p,
        "gcq": mla_q_norm_g[l][None, :],
        "gckv": mla_kv_norm_g[l][None, :],
        "w_uq": w_uq_p, "w_uq_sw": w_uq_sw, "w_uk": w_uk_p, "w_uv_t": w_uv_t,
        "q_gc1": _rope_lane_vec(qn[MLA_NOPE:]) * q_scale, "q_gc0": nope_vec(qn) * q_scale,
        "q_gs": _rope_lane_vec(_pair_swapped(qn[MLA_NOPE:])) * q_scale,
        "k_gc1": _rope_lane_vec(kn[MLA_NOPE:]), "k_gc0": nope_vec(kn),
        "k_gs": _rope_lane_vec(_pair_swapped(kn[MLA_NOPE:])),
        "dq_gc": dqn * d_scale, "dq_gs": _pair_swapped(dqn) * d_scale,
        "dk_gc": dkn, "dk_gs": _pair_swapped(dkn),
        "mqn": jnp.tile(mem_qn_g[l], MEM_HEADS)[None, :] * (1.0 / math.sqrt(MEM_D)),
        "bd_diff": _block_diag_ones(DIFF_D), "bd_mem": _block_diag_ones(MEM_D),
    }


def kernel(x, mem, positions, norm_g, w_in, mla_q_norm_g, mla_kv_norm_g, w_uq, w_ukv, mla_qn_g, mla_kn_g,
           diff_qn_g, diff_kn_g, diff_lambda, diff_subln_g, mem_norm_g, w_mem_kv, mem_qn_g, mem_kn_g, w_out):
    batch, seq, d_model = x.shape
    depth = w_in.shape[0]
    assert d_model == D_MODEL and MLA_V == DIFF_V and TQ_MLA == TK and TQ_DIFF == TK
    assert seq % max(TM_PROJ, TQ_MLA, TQ_DIFF, TM_OUT) == 0 and (batch * seq) % 1024 == 0
    cos_t, sin_t = _rope_tables(positions)
    kmem_all, vmem_all = _mem_kv(mem, mem_norm_g, w_mem_kv, mem_kn_g, _block_diag_ones(MEM_D))
    x2d = x.reshape(batch * seq, D_MODEL)
    for l in range(depth):
        lam_init = 0.8 - 0.6 * math.exp(-0.3 * l)
        p = _layer_params(l, norm_g, w_in, mla_q_norm_g, mla_kv_norm_g, w_uq, w_ukv, mla_qn_g, mla_kn_g,
                          diff_qn_g, diff_kn_g, mem_qn_g)
        q, k, v, dq, dk, dv, y_mem, gate = _proj_call(x2d, cos_t, sin_t, kmem_all[l], vmem_all[l], p, seq)
        y_mla = _mla_attn_call(q, k, v, gate, batch, seq)
        lam_pad = jnp.pad(diff_lambda[l], ((0, 4), (0, LANES - DIFF_D)))
        subln_t = jnp.tile(diff_subln_g[l], 2)[None, :]
        y_diff = _diff_attn_call(dq, dk, dv, gate, lam_pad, subln_t, batch, seq, lam_init)
        x2d = _out_call(x2d, y_mla, y_diff, y_mem, w_out[l].astype(BF16))
    return x2d.reshape(batch, seq, D_MODEL)
```

```python
import functools
import math

import numpy as np
import jax
import jax.numpy as jnp
from jax import lax
from jax.experimental import pallas as pl
from jax.experimental.pallas import tpu as pltpu

F32 = jnp.float32
BF16 = jnp.bfloat16

D_MODEL = 1024
ROPE_THETA = 10000.0
RMS_EPS = 1e-6
NEG_INF = -1e30
LOG2_E = math.log2(math.e)

MLA_HEADS = 8
Q_LORA = 256
KV_LORA = 128
MLA_NOPE = 64
MLA_ROPE = 32
MLA_QK = MLA_NOPE + MLA_ROPE
MLA_V = 64
MLA_W = MLA_HEADS * MLA_V

DIFF_HEADS = 4
DIFF_D = 32
DIFF_V = 2 * DIFF_D
DIFF_W = DIFF_HEADS * DIFF_V
DIFF_QK_W = 2 * DIFF_HEADS * DIFF_D

MEM_HEADS = 4
MEM_D = 64
MEM_W = MEM_HEADS * MEM_D

D_MIX = MLA_W + DIFF_W + MEM_W

LANES = 128
MXU_DIM = 256
ROPE_D = 32
HEAD_PAD = LANES

_C_CQ = 0
_C_CKV = _C_CQ + Q_LORA
_C_KR = _C_CKV + KV_LORA
_C_KR_SW = _C_KR + LANES
_C_DQ = _C_KR_SW + LANES
_C_DQ_SW = _C_DQ + DIFF_QK_W
_C_DK = _C_DQ_SW + DIFF_QK_W
_C_DK_SW = _C_DK + DIFF_QK_W
_C_DV = _C_DK_SW + DIFF_QK_W
_C_MQ = _C_DV + DIFF_HEADS * LANES
_C_Z = _C_MQ + MEM_W
_C_END = _C_Z + D_MIX

VMEM_LIMIT = 56 * 1024 * 1024

V_DIM = MLA_V
VT_ROWS = 80

TK = 512
TM_PROJ = TK
TM_OUT = 512
TQ_MLA = 512
TQ_DIFF = 512
SCORE_SLOTS = 2
TRIP_STEPS = 4
SCORE_CEIL = 3.0e38

_NT = (((1,), (1,)), ((), ()))


def _rms(x):
    return x * lax.rsqrt(jnp.mean(x * x, axis=-1, keepdims=True) + RMS_EPS)


def _lane_iota(shape):
    return lax.broadcasted_iota(jnp.int32, shape, len(shape) - 1)


def _seg_sum_sq(x, bd_ref):
    sq = x * x
    hi = sq.astype(BF16)
    lo = (sq - hi.astype(F32)).astype(BF16)
    bd = bd_ref[...]
    return jnp.dot(hi, bd, preferred_element_type=F32) + jnp.dot(lo, bd, preferred_element_type=F32)


def _rope_table_kernel(pos_ref, inv_ref, cos_ref, sin_ref):
    ang = pos_ref[...].astype(F32) * inv_ref[...]
    lane = _lane_iota(ang.shape)
    sign = jnp.where((lane & (ROPE_D // 2)) == 0, -1.0, 1.0)
    cos_ref[...] = jnp.cos(ang)
    sin_ref[...] = jnp.sin(ang) * sign


def _rope_tables(positions):
    n = positions.size
    tm = 1024
    inv = ROPE_THETA ** (-jnp.arange(0, ROPE_D, 2, dtype=F32) / ROPE_D)
    inv_t = jnp.tile(inv, LANES // (ROPE_D // 2))[None, :]
    pos = positions.reshape(n, 1)
    return pl.pallas_call(
        _rope_table_kernel,
        grid=(n // tm,),
        in_specs=[pl.BlockSpec((tm, 1), lambda i: (i, 0)),
                  pl.BlockSpec((1, LANES), lambda i: (0, 0))],
        out_specs=[pl.BlockSpec((tm, LANES), lambda i: (i, 0)),
                   pl.BlockSpec((tm, LANES), lambda i: (i, 0))],
        out_shape=[jax.ShapeDtypeStruct((n, LANES), F32)] * 2,
        name="rope_tables",
    )(pos, inv_t)


def _mem_kv_kernel(mem_ref, g_ref, w_ref, kn_ref, bd_ref, k_ref, v_ref):
    m = mem_ref[0]
    mn = (_rms(m) * g_ref[0]).astype(BF16)
    kv = jnp.dot(mn, w_ref[0], preferred_element_type=F32)
    k = kv[:, :MEM_W]
    k = k * lax.rsqrt(_seg_sum_sq(k, bd_ref) * (1.0 / MEM_D) + RMS_EPS) * kn_ref[0]
    k_ref[0, 0] = k.astype(BF16)
    v_ref[0, 0] = kv[:, MEM_W:].astype(BF16)


def _mem_kv(mem, mem_norm_g, w_mem_kv, mem_kn_g, bd_mem):
    depth = w_mem_kv.shape[0]
    b, m, d = mem.shape
    kn_t = jnp.tile(mem_kn_g, (1, MEM_HEADS))[:, None, :]
    return pl.pallas_call(
        _mem_kv_kernel,
        grid=(depth, b),
        in_specs=[pl.BlockSpec((1, m, d), lambda l, i: (i, 0, 0)),
                  pl.BlockSpec((1, 1, d), lambda l, i: (l, 0, 0)),
                  pl.BlockSpec((1, d, 2 * MEM_W), lambda l, i: (l, 0, 0)),
                  pl.BlockSpec((1, 1, MEM_W), lambda l, i: (l, 0, 0)),
                  pl.BlockSpec(bd_mem.shape, lambda l, i: (0, 0))],
        out_specs=[pl.BlockSpec((1, 1, m, MEM_W), lambda l, i: (l, i, 0, 0)),
                   pl.BlockSpec((1, 1, m, MEM_W), lambda l, i: (l, i, 0, 0))],
        out_shape=[jax.ShapeDtypeStruct((depth, b, m, MEM_W), BF16)] * 2,
        name="mem_kv",
    )(mem, mem_norm_g[:, None, :], w_mem_kv.astype(BF16), kn_t, bd_mem)


_PROJ_PARAMS = ("gx", "w_in", "gcq", "gckv", "w_uq", "w_uq_sw", "w_uk", "w_uv_t",
                "q_gc1", "q_gc0", "q_gs", "k_gc1", "k_gc0", "k_gs",
                "dq_gc", "dq_gs", "dk_gc", "dk_gs", "mqn", "bd_diff", "bd_mem")


def _proj_kernel(x_ref, cos_ref, sin_ref, gx_ref, win_ref, gcq_ref, gckv_ref, wuq_ref, wuqsw_ref, wuk_ref, wuvt_ref,
                 qgc1_ref, qgc0_ref, qgs_ref, kgc1_ref, kgc0_ref, kgs_ref,
                 dqgc_ref, dqgs_ref, dkgc_ref, dkgs_ref, mqn_ref, bdd_ref, bdm_ref, kmem_ref, vmem_ref,
                 q_out, k_out, v_out, dq_out, dk_out, dv_out, ymem_out, gate_out):
    tm = x_ref.shape[0]
    h = (_rms(x_ref[...]) * gx_ref[...]).astype(BF16)
    proj = jnp.dot(h, win_ref[...], preferred_element_type=F32)
    cos = cos_ref[...]
    sin = sin_ref[...]

    cqn = (_rms(proj[:, _C_CQ:_C_CQ + Q_LORA]) * gcq_ref[...]).astype(BF16)
    q = jnp.dot(cqn, wuq_ref[...], preferred_element_type=F32)
    q_sw = jnp.dot(cqn, wuqsw_ref[...], preferred_element_type=F32)
    q_gc = cos * qgc1_ref[...] + qgc0_ref[...]
    q_gs = sin * qgs_ref[...]
    for hd in range(MLA_HEADS):
        sl = slice(hd * HEAD_PAD, (hd + 1) * HEAD_PAD)
        qh = q[:, sl]
        r = lax.rsqrt(jnp.sum(qh * qh, axis=-1, keepdims=True) * (1.0 / MLA_QK) + RMS_EPS)
        q_out[:, sl] = ((qh * q_gc + q_sw[:, sl] * q_gs) * r).astype(BF16)

    ckv = _rms(proj[:, _C_CKV:_C_CKV + KV_LORA]) * gckv_ref[...]
    ckvn = ckv.astype(BF16)
    k_nope = jnp.dot(ckvn, wuk_ref[...], preferred_element_type=F32)
    kr = proj[:, _C_KR:_C_KR + LANES]
    k_gc = cos * kgc1_ref[...] + kgc0_ref[...]
    kr_rot = proj[:, _C_KR_SW:_C_KR_SW + LANES] * (sin * kgs_ref[...])
    for hd in range(MLA_HEADS):
        sl = slice(hd * HEAD_PAD, (hd + 1) * HEAD_PAD)
        kh = k_nope[:, sl] + kr
        r = lax.rsqrt(jnp.sum(kh * kh, axis=-1, keepdims=True) * (1.0 / MLA_QK) + RMS_EPS)
        k_out[:, sl] = ((kh * k_gc + kr_rot) * r).astype(BF16)
    vt = jnp.dot(wuvt_ref[...], ckv.T.astype(BF16), preferred_element_type=F32)
    ones_row = (lax.broadcasted_iota(jnp.int32, (VT_ROWS, tm), 0) == V_DIM).astype(F32)
    for hd in range(MLA_HEADS):
        rows = slice(hd * VT_ROWS, (hd + 1) * VT_ROWS)
        v_out[0, rows, :] = (vt[rows] + ones_row).astype(BF16)

    for raw_c, sw_c, gc_ref, gs_ref, out in ((_C_DQ, _C_DQ_SW, dqgc_ref, dqgs_ref, dq_out),
                                             (_C_DK, _C_DK_SW, dkgc_ref, dkgs_ref, dk_out)):
        raw = proj[:, raw_c:raw_c + DIFF_QK_W]
        r = lax.rsqrt(_seg_sum_sq(raw, bdd_ref) * (1.0 / DIFF_D) + RMS_EPS)
        for c in range(DIFF_QK_W // LANES):
            sl = slice(c * LANES, (c + 1) * LANES)
            rot = raw[:, sl] * (cos * gc_ref[:, sl]) + proj[:, sw_c + c * LANES:sw_c + (c + 1) * LANES] * (sin * gs_ref[:, sl])
            out[:, sl] = (rot * r[:, sl]).astype(BF16)
    ones_lane = (_lane_iota((1, LANES)) == V_DIM).astype(F32)
    for hd in range(DIFF_HEADS):
        dv_t = (proj[:, _C_DV + hd * LANES:_C_DV + (hd + 1) * LANES] + ones_lane).T
        dv_out[0, hd * VT_ROWS:(hd + 1) * VT_ROWS, :] = dv_t[:VT_ROWS].astype(BF16)

    z = proj[:, _C_Z:_C_END]
    gate = z * (1.0 / (1.0 + jnp.exp(-z)))
    gate_out[...] = gate[:, :MLA_W + DIFF_W].astype(BF16)

    mq = proj[:, _C_MQ:_C_MQ + MEM_W]
    mq = mq * lax.rsqrt(_seg_sum_sq(mq, bdm_ref) * (1.0 / MEM_D) + RMS_EPS) * mqn_ref[...]
    kmem = kmem_ref[0]
    vmem = vmem_ref[0]
    lane_w = _lane_iota((1, MEM_W))
    o = jnp.zeros(mq.shape, F32)
    for hd in range(MEM_HEADS):
        head_mask = (lane_w // MEM_D) == hd
        qh = jnp.where(head_mask, mq, 0.0).astype(BF16)
        s = lax.dot_general(qh, kmem, _NT, preferred_element_type=F32)
        p = jnp.exp(s - jnp.max(s, axis=-1, keepdims=True))
        inv_l = 1.0 / jnp.sum(p, axis=-1, keepdims=True)
        vh = jnp.where(head_mask, vmem, jnp.zeros_like(vmem))
        o = o + jnp.dot(p.astype(BF16), vh, preferred_element_type=F32) * inv_l
    ymem_out[...] = (o * gate[:, MLA_W + DIFF_W:]).astype(BF16)


def _proj_call(x2d, cos_t, sin_t, kmem, vmem, p, seq):
    n = x2d.shape[0]
    tm = TM_PROJ
    per_batch = seq // tm
    row = lambda i: (i, 0)
    const = lambda i: (0, 0)
    params = [p[name] for name in _PROJ_PARAMS]
    mem_spec = pl.BlockSpec((1,) + kmem.shape[1:], lambda i: (i // per_batch, 0, 0))
    in_specs = ([pl.BlockSpec((tm, D_MODEL), row), pl.BlockSpec((tm, LANES), row), pl.BlockSpec((tm, LANES), row)]
                + [pl.BlockSpec(a.shape, const) for a in params] + [mem_spec, mem_spec])

    def rows_out(width):
        return pl.BlockSpec((tm, width), row), jax.ShapeDtypeStruct((n, width), BF16)

    def values_t_out(heads):
        return (pl.BlockSpec((1, heads * VT_ROWS, tm), lambda i: (i, 0, 0)),
                jax.ShapeDtypeStruct((n // tm, heads * VT_ROWS, tm), BF16))

    outs = [rows_out(MLA_HEADS * HEAD_PAD), rows_out(MLA_HEADS * HEAD_PAD), values_t_out(MLA_HEADS),
            rows_out(DIFF_QK_W), rows_out(DIFF_QK_W), values_t_out(DIFF_HEADS),
            rows_out(MEM_W), rows_out(MLA_W + DIFF_W)]
    return pl.pallas_call(
        _proj_kernel,
        grid=(n // tm,),
        in_specs=in_specs,
        out_specs=[o[0] for o in outs],
        out_shape=[o[1] for o in outs],
        compiler_params=pltpu.CompilerParams(dimension_semantics=("arbitrary",), vmem_limit_bytes=VMEM_LIMIT),
        name="proj_prep",
    )(x2d, cos_t, sin_t, *params, kmem, vmem)


def _causal_keep_t(tk, tq, key_offset):
    r = lax.broadcasted_iota(jnp.int32, (tk, tq), 0)
    c = lax.broadcasted_iota(jnp.int32, (tk, tq), 1)
    return (r + key_offset) <= c


def _init_limits(lim_ref, tk, tq):
    lim_ref[0] = jnp.full((tk, tq), SCORE_CEIL, F32)
    lim_ref[1] = jnp.where(_causal_keep_t(tk, tq, 0), SCORE_CEIL, NEG_INF)


def _strips(tq):
    return [slice(c, c + MXU_DIM) for c in range(0, tq, MXU_DIM)]


def _score_strip(k_blk, q_strip, lim_ref, diag, slot, idx, cols):
    s_ref, mb_ref = slot
    s = lax.dot_general(k_blk, q_strip, _NT, preferred_element_type=F32)
    s = jnp.minimum(s, lim_ref[diag, :, cols])
    s_ref[idx, :, cols] = s
    mb_ref[idx, :, cols] = jnp.max(s, axis=0, keepdims=True)


def _softmax_pv_strip(slot, idx, cols, vt_blk, first, m_ref, acc_ref, qi):
    s_ref, mb_ref = slot
    s = s_ref[idx, :, cols]
    m_old = jnp.where(first, NEG_INF, m_ref[idx, :, cols])
    m_new = jnp.maximum(m_old, mb_ref[idx, :, cols])
    alpha = jnp.exp2(m_old - m_new)
    p = jnp.exp2(s - m_new).astype(BF16)
    acc_ref[qi, idx, :, cols] = (acc_ref[qi, idx, :, cols] * alpha
                                 + jnp.dot(vt_blk, p, preferred_element_type=F32))
    m_ref[idx, :, cols] = m_new


def _causal_steps(nq):
    qi = np.concatenate([np.full(i + 1, i, np.int32) for i in range(nq)])
    kj = np.concatenate([np.arange(i + 1, dtype=np.int32) for i in range(nq)])
    return qi, kj


def _flat_sweep(n_steps, scores, update):
    def pair(t0, u, with_scores):
        ahead = scores(t0 + u + 1, (u + 1) % SCORE_SLOTS) if with_scores else []
        now = update(t0 + u, u % SCORE_SLOTS)
        for i in range(max(len(ahead), len(now))):
            if i < len(ahead):
                ahead[i]()
            if i < len(now):
                now[i]()

    for piece in scores(0, 0):
        piece()
    trips = (n_steps - 1) // TRIP_STEPS

    def body(i, carry):
        for u in range(TRIP_STEPS):
            pair(i * TRIP_STEPS, u, True)
        return carry

    lax.fori_loop(0, trips, body, 0)
    done = trips * TRIP_STEPS
    for u in range(n_steps - done):
        pair(done, u, done + u + 1 < n_steps)


def _normalised_t(acc):
    return acc[:V_DIM] * (1.0 / acc[V_DIM:V_DIM + 1])


def _attn_scratch(count, nq, tk, tq):
    shapes = []
    for _ in range(SCORE_SLOTS):
        shapes += [pltpu.VMEM((count, tk, tq), F32), pltpu.VMEM((count, 1, tq), F32)]
    return shapes + [pltpu.VMEM((2, tk, tq), F32), pltpu.VMEM((count, 1, tq), F32),
                     pltpu.VMEM((nq, count, VT_ROWS, tq), F32)]


def _split_scratch(scratch):
    slots = [(scratch[2 * i], scratch[2 * i + 1]) for i in range(SCORE_SLOTS)]
    lim_ref, m_ref, acc_ref = scratch[2 * SCORE_SLOTS:]
    return slots, lim_ref, m_ref, acc_ref


def _mla_attn_kernel(qi_tab, kj_tab, q_ref, k_ref, vt_ref, gate_ref, o_ref, *scratch, tq, tk, nq):
    slots, lim_ref, m_ref, acc_ref = _split_scratch(scratch)
    _init_limits(lim_ref, tk, tq)
    acc_ref[...] = jnp.zeros(acc_ref.shape, F32)

    def scores(t, slot):
        qi, kj = qi_tab[t], kj_tab[t]
        krows = pl.ds(pl.multiple_of(kj * tk, tk), tk)
        diag = (kj == qi).astype(jnp.int32)

        def piece(hd, cols):
            sl = slice(hd * HEAD_PAD, (hd + 1) * HEAD_PAD)
            qrows = pl.ds(pl.multiple_of(qi * tq + cols.start, MXU_DIM), MXU_DIM)
            _score_strip(k_ref[krows, sl], q_ref[qrows, sl], lim_ref, diag, slots[slot], hd, cols)

        return [functools.partial(piece, hd, cols) for hd in range(2) for cols in _strips(tq)]

    def update(t, slot):
        qi, kj = qi_tab[t], kj_tab[t]

        def piece(hd, cols):
            _softmax_pv_strip(slots[slot], hd, cols, vt_ref[kj, hd * VT_ROWS:(hd + 1) * VT_ROWS, :], kj == 0,
                              m_ref, acc_ref, qi)

        return [functools.partial(piece, hd, cols) for hd in range(2) for cols in _strips(tq)]

    _flat_sweep(nq * (nq + 1) // 2, scores, update)

    def finish(qb, carry):
        rows = pl.ds(pl.multiple_of(qb * tq, tq), tq)
        o_t = jnp.concatenate([_normalised_t(acc_ref[qb, 0]), _normalised_t(acc_ref[qb, 1])], axis=0)
        o_ref[rows, :] = (o_t.T * gate_ref[rows, :].astype(F32)).astype(BF16)
        return carry

    lax.fori_loop(0, nq, finish, 0)


def _mla_attn_call(q, k, vt, gate, batch, seq):
    tq, tk = TQ_MLA, TK
    nq = seq // tq
    pairs = MLA_HEADS // 2
    qi_tab, kj_tab = _causal_steps(nq)
    blk = lambda b, hp, *_: (b, hp)
    grid_spec = pltpu.PrefetchScalarGridSpec(
        num_scalar_prefetch=2,
        grid=(batch, pairs),
        in_specs=[pl.BlockSpec((seq, 2 * HEAD_PAD), blk),
                  pl.BlockSpec((seq, 2 * HEAD_PAD), blk),
                  pl.BlockSpec((seq // tk, 2 * VT_ROWS, tk), lambda b, hp, *_: (b, hp, 0)),
                  pl.BlockSpec((seq, LANES), blk)],
        out_specs=pl.BlockSpec((seq, LANES), blk),
        scratch_shapes=_attn_scratch(2, nq, tk, tq))
    return pl.pallas_call(
        functools.partial(_mla_attn_kernel, tq=tq, tk=tk, nq=nq),
        grid_spec=grid_spec,
        out_shape=jax.ShapeDtypeStruct((batch * seq, MLA_W), BF16),
        compiler_params=pltpu.CompilerParams(dimension_semantics=("arbitrary",) * 2, vmem_limit_bytes=VMEM_LIMIT),
        name="mla_attn",
    )(qi_tab, kj_tab, q, k, vt, gate)


def _diff_attn_kernel(qi_tab, kj_tab, q_ref, k_ref, vt_ref, gate_ref, lam_ref, sg_ref, o_ref, *scratch,
                      tq, tk, nq, lam_init):
    slots, lim_ref, m_ref, acc_ref = _split_scratch(scratch)
    _init_limits(lim_ref, tk, tq)
    acc_ref[...] = jnp.zeros(acc_ref.shape, F32)
    map_of_lane = _lane_iota((1, LANES)) // DIFF_D

    def scores(t, slot):
        qi, kj = qi_tab[t], kj_tab[t]
        krows = pl.ds(pl.multiple_of(kj * tk, tk), tk)
        diag = (kj == qi).astype(jnp.int32)

        def piece(idx, cols):
            q_strip = q_ref[pl.ds(pl.multiple_of(qi * tq + cols.start, MXU_DIM), MXU_DIM), :]
            q_map = jnp.where(map_of_lane == idx, q_strip, jnp.zeros_like(q_strip))
            _score_strip(k_ref[krows, :], q_map, lim_ref, diag, slots[slot], idx, cols)

        return [functools.partial(piece, idx, cols) for idx in range(4) for cols in _strips(tq)]

    def update(t, slot):
        qi, kj = qi_tab[t], kj_tab[t]

        def piece(idx, cols):
            hd = idx // 2
            _softmax_pv_strip(slots[slot], idx, cols, vt_ref[kj, hd * VT_ROWS:(hd + 1) * VT_ROWS, :], kj == 0,
                              m_ref, acc_ref, qi)

        return [functools.partial(piece, idx, cols) for idx in range(4) for cols in _strips(tq)]

    _flat_sweep(nq * (nq + 1) // 2, scores, update)

    lv = lam_ref[...]
    lam = (jnp.exp(jnp.sum(lv[0:1] * lv[1:2], axis=-1, keepdims=True))
           - jnp.exp(jnp.sum(lv[2:3] * lv[3:4], axis=-1, keepdims=True)) + lam_init)

    def finish(qb, carry):
        rows = pl.ds(pl.multiple_of(qb * tq, tq), tq)
        heads_t = []
        for hd in range(2):
            o = _normalised_t(acc_ref[qb, 2 * hd]) - lam * _normalised_t(acc_ref[qb, 2 * hd + 1])
            heads_t.append(o * lax.rsqrt(jnp.mean(o * o, axis=0, keepdims=True) + RMS_EPS))
        o = jnp.concatenate(heads_t, axis=0).T * sg_ref[...] * (1.0 - lam_init)
        o_ref[rows, :] = (o * gate_ref[rows, :].astype(F32)).astype(BF16)
        return carry

    lax.fori_loop(0, nq, finish, 0)


def _diff_attn_call(q, k, vt, gate, lam_pad, subln_t, batch, seq, lam_init):
    tq, tk = TQ_DIFF, TK
    nq = seq // tq
    pairs = DIFF_HEADS // 2
    gate_off = MLA_W // LANES
    qi_tab, kj_tab = _causal_steps(nq)
    blk = lambda b, hp, *_: (b, hp)
    const = lambda b, hp, *_: (0, 0)
    grid_spec = pltpu.PrefetchScalarGridSpec(
        num_scalar_prefetch=2,
        grid=(batch, pairs),
        in_specs=[pl.BlockSpec((seq, LANES), blk),
                  pl.BlockSpec((seq, LANES), blk),
                  pl.BlockSpec((seq // tk, 2 * VT_ROWS, tk), lambda b, hp, *_: (b, hp, 0)),
                  pl.BlockSpec((seq, LANES), lambda b, hp, *_: (b, gate_off + hp)),
                  pl.BlockSpec(lam_pad.shape, const),
                  pl.BlockSpec(subln_t.shape, const)],
        out_specs=pl.BlockSpec((seq, LANES), blk),
        scratch_shapes=_attn_scratch(4, nq, tk, tq))
    return pl.pallas_call(
        functools.partial(_diff_attn_kernel, tq=tq, tk=tk, nq=nq, lam_init=lam_init),
        grid_spec=grid_spec,
        out_shape=jax.ShapeDtypeStruct((batch * seq, DIFF_W), BF16),
        compiler_params=pltpu.CompilerParams(dimension_semantics=("arbitrary",) * 2, vmem_limit_bytes=VMEM_LIMIT),
        name="diff_attn",
    )(qi_tab, kj_tab, q, k, vt, gate, lam_pad, subln_t)


def _out_kernel(x_ref, ymla_ref, ydiff_ref, ymem_ref, w_ref, o_ref):
    y = jnp.concatenate([ymla_ref[...], ydiff_ref[...], ymem_ref[...]], axis=1)
    o_ref[...] = x_ref[...] + jnp.dot(y, w_ref[...], preferred_element_type=F32)


def _out_call(x2d, y_mla, y_diff, y_mem, w_out):
    n = x2d.shape[0]
    tm = TM_OUT
    row = lambda i: (i, 0)
    return pl.pallas_call(
        _out_kernel,
        grid=(n // tm,),
        in_specs=[pl.BlockSpec((tm, D_MODEL), row), pl.BlockSpec((tm, MLA_W), row),
                  pl.BlockSpec((tm, DIFF_W), row), pl.BlockSpec((tm, MEM_W), row),
                  pl.BlockSpec(w_out.shape, lambda i: (0, 0))],
        out_specs=pl.BlockSpec((tm, D_MODEL), row),
        out_shape=jax.ShapeDtypeStruct((n, D_MODEL), F32),
        compiler_params=pltpu.CompilerParams(dimension_semantics=("arbitrary",), vmem_limit_bytes=VMEM_LIMIT),
        name="out_proj",
    )(x2d, y_mla, y_diff, y_mem, w_out)


def _block_diag_ones(seg):
    idx = np.arange(MXU_DIM) // seg
    return jnp.asarray(idx[:, None] == idx[None, :], dtype=BF16)


def _pair_swapped(a):
    lead = a.shape[:-1]
    a4 = a.reshape(lead + (a.shape[-1] // ROPE_D, 2, ROPE_D // 2))
    return a4[..., ::-1, :].reshape(a.shape)


def _head_padded(a, heads, width, offset=0):
    rows = a.shape[0]
    a3 = a.reshape(rows, heads, width)
    return jnp.pad(a3, ((0, 0), (0, 0), (offset, LANES - width - offset))).reshape(rows, heads * LANES)


def _rope_lane_vec(g_rope):
    return jnp.pad(g_rope, (MLA_NOPE, LANES - MLA_QK))[None, :]


def _layer_params(l, norm_g, w_in, mla_q_norm_g, mla_kv_norm_g, w_uq, w_ukv, mla_qn_g, mla_kn_g,
                  diff_qn_g, diff_kn_g, mem_qn_g):
    w = w_in[l]
    o_cq, o_ckv, o_kr = 0, Q_LORA, Q_LORA + KV_LORA
    o_dq = o_kr + MLA_ROPE
    o_dk = o_dq + DIFF_QK_W
    o_dv = o_dk + DIFF_QK_W
    o_mq = o_dv + DIFF_W
    o_z = o_mq + MEM_W
    w_kr, w_dq, w_dk = w[:, o_kr:o_dq], w[:, o_dq:o_dk], w[:, o_dk:o_dv]
    w_in_p = jnp.concatenate([
        w[:, o_cq:o_ckv], w[:, o_ckv:o_kr],
        _head_padded(w_kr, 1, MLA_ROPE, MLA_NOPE), _head_padded(_pair_swapped(w_kr), 1, MLA_ROPE, MLA_NOPE),
        w_dq, _pair_swapped(w_dq), w_dk, _pair_swapped(w_dk),
        _head_padded(w[:, o_dv:o_mq], DIFF_HEADS, DIFF_V),
        w[:, o_mq:o_z], w[:, o_z:]], axis=1).astype(BF16)

    uq3 = w_uq[l].reshape(Q_LORA, MLA_HEADS, MLA_QK)
    w_uq_p = _head_padded(w_uq[l], MLA_HEADS, MLA_QK).astype(BF16)
    w_uq_sw = _head_padded(_pair_swapped(uq3[:, :, MLA_NOPE:]).reshape(Q_LORA, MLA_HEADS * MLA_ROPE),
                           MLA_HEADS, MLA_ROPE, MLA_NOPE).astype(BF16)
    kv3 = w_ukv[l].reshape(KV_LORA, MLA_HEADS, MLA_NOPE + MLA_V)
    w_uk_p = _head_padded(kv3[:, :, :MLA_NOPE].reshape(KV_LORA, MLA_HEADS * MLA_NOPE), MLA_HEADS, MLA_NOPE).astype(BF16)
    w_uv_t = jnp.pad(jnp.transpose(kv3[:, :, MLA_NOPE:], (1, 2, 0)),
                     ((0, 0), (0, VT_ROWS - MLA_V), (0, 0))).reshape(MLA_HEADS * VT_ROWS, KV_LORA).astype(BF16)

    q_scale = LOG2_E / math.sqrt(MLA_QK)
    d_scale = LOG2_E / math.sqrt(DIFF_D)
    qn, kn = mla_qn_g[l], mla_kn_g[l]
    nope_vec = lambda g: jnp.pad(g[:MLA_NOPE], (0, LANES - MLA_NOPE))[None, :]
    dqn = jnp.tile(diff_qn_g[l], 2 * DIFF_HEADS)[None, :]
    dkn = jnp.tile(diff_kn_g[l], 2 * DIFF_HEADS)[None, :]
    return {
        "gx": norm_g[l][None, :],
        "w_in": w_in_p,
        "gcq": mla_q_norm_g[l][None, :],
        "gckv": mla_kv_norm_g[l][None, :],
        "w_uq": w_uq_p, "w_uq_sw": w_uq_sw, "w_uk": w_uk_p, "w_uv_t": w_uv_t,
        "q_gc1": _rope_lane_vec(qn[MLA_NOPE:]) * q_scale, "q_gc0": nope_vec(qn) * q_scale,
        "q_gs": _rope_lane_vec(_pair_swapped(qn[MLA_NOPE:])) * q_scale,
        "k_gc1": _rope_lane_vec(kn[MLA_NOPE:]), "k_gc0": nope_vec(kn),
        "k_gs": _rope_lane_vec(_pair_swapped(kn[MLA_NOPE:])),
        "dq_gc": dqn * d_scale, "dq_gs": _pair_swapped(dqn) * d_scale,
        "dk_gc": dkn, "dk_gs": _pair_swapped(dkn),
        "mqn": jnp.tile(mem_qn_g[l], MEM_HEADS)[None, :] * (1.0 / math.sqrt(MEM_D)),
        "bd_diff": _block_diag_ones(DIFF_D), "bd_mem": _block_diag_ones(MEM_D),
    }


def kernel(x, mem, positions, norm_g, w_in, mla_q_norm_g, mla_kv_norm_g, w_uq, w_ukv, mla_qn_g, mla_kn_g,
           diff_qn_g, diff_kn_g, diff_lambda, diff_subln_g, mem_norm_g, w_mem_kv, mem_qn_g, mem_kn_g, w_out):
    batch, seq, d_model = x.shape
    depth = w_in.shape[0]
    assert d_model == D_MODEL and MLA_V == DIFF_V and TQ_MLA == TK and TQ_DIFF == TK
    assert TRIP_STEPS % SCORE_SLOTS == 0 and SCORE_SLOTS >= 2 and TK % MXU_DIM == 0
    assert seq % max(TM_PROJ, TQ_MLA, TQ_DIFF, TM_OUT) == 0 and (batch * seq) % 1024 == 0
    cos_t, sin_t = _rope_tables(positions)
    kmem_all, vmem_all = _mem_kv(mem, mem_norm_g, w_mem_kv, mem_kn_g, _block_diag_ones(MEM_D))
    x2d = x.reshape(batch * seq, D_MODEL)
    for l in range(depth):
        lam_init = 0.8 - 0.6 * math.exp(-0.3 * l)
        p = _layer_params(l, norm_g, w_in, mla_q_norm_g, mla_kv_norm_g, w_uq, w_ukv, mla_qn_g, mla_kn_g,
                          diff_qn_g, diff_kn_g, mem_qn_g)
        q, k, v, dq, dk, dv, y_mem, gate = _proj_call(x2d, cos_t, sin_t, kmem_all[l], vmem_all[l], p, seq)
        y_mla = _mla_attn_call(q, k, v, gate, batch, seq)
        lam_pad = jnp.pad(diff_lambda[l], ((0, 4), (0, LANES - DIFF_D)))
        subln_t = jnp.tile(diff_subln_g[l], 2)[None, :]
        y_diff = _diff_attn_call(dq, dk, dv, gate, lam_pad, subln_t, batch, seq, lam_init)
        x2d = _out_call(x2d, y_mla, y_diff, y_mem, w_out[l].astype(BF16))
    return x2d.reshape(batch, seq, D_MODEL)
```

```python
import functools
import math

import numpy as np
import jax
import jax.numpy as jnp
from jax import lax
from jax.experimental import pallas as pl
from jax.experimental.pallas import tpu as pltpu

F32 = jnp.float32
BF16 = jnp.bfloat16

D_MODEL = 1024
ROPE_THETA = 10000.0
RMS_EPS = 1e-6
NEG_INF = -1e30
LOG2_E = math.log2(math.e)

MLA_HEADS = 8
Q_LORA = 256
KV_LORA = 128
MLA_NOPE = 64
MLA_ROPE = 32
MLA_QK = MLA_NOPE + MLA_ROPE
MLA_V = 64
MLA_W = MLA_HEADS * MLA_V

DIFF_HEADS = 4
DIFF_D = 32
DIFF_V = 2 * DIFF_D
DIFF_W = DIFF_HEADS * DIFF_V
DIFF_QK_W = 2 * DIFF_HEADS * DIFF_D

MEM_HEADS = 4
MEM_D = 64
MEM_W = MEM_HEADS * MEM_D

D_MIX = MLA_W + DIFF_W + MEM_W

LANES = 128
MXU_DIM = 256
ROPE_D = 32
HEAD_PAD = LANES

_C_CQ = 0
_C_CKV = _C_CQ + Q_LORA
_C_KR = _C_CKV + KV_LORA
_C_KR_SW = _C_KR + LANES
_C_DQ = _C_KR_SW + LANES
_C_DQ_SW = _C_DQ + DIFF_QK_W
_C_DK = _C_DQ_SW + DIFF_QK_W
_C_DK_SW = _C_DK + DIFF_QK_W
_C_DV = _C_DK_SW + DIFF_QK_W
_C_MQ = _C_DV + DIFF_HEADS * LANES
_C_Z = _C_MQ + MEM_W
_C_END = _C_Z + D_MIX

VMEM_LIMIT = 56 * 1024 * 1024

V_DIM = MLA_V
VT_ROWS = 80

TK = 512
TM_PROJ = TK
TM_OUT = 512
TQ_MLA = 512
TQ_DIFF = 512
SCORE_SLOTS = 2
TRIP_STEPS = 4
SCORE_CEIL = 3.0e38

_NT = (((1,), (1,)), ((), ()))


def _rms(x):
    return x * lax.rsqrt(jnp.mean(x * x, axis=-1, keepdims=True) + RMS_EPS)


def _lane_iota(shape):
    return lax.broadcasted_iota(jnp.int32, shape, len(shape) - 1)


def _seg_sum_sq(x, bd_ref):
    sq = x * x
    hi = sq.astype(BF16)
    lo = (sq - hi.astype(F32)).astype(BF16)
    bd = bd_ref[...]
    return jnp.dot(hi, bd, preferred_element_type=F32) + jnp.dot(lo, bd, preferred_element_type=F32)


def _rope_table_kernel(pos_ref, inv_ref, cos_ref, sin_ref):
    ang = pos_ref[...].astype(F32) * inv_ref[...]
    lane = _lane_iota(ang.shape)
    sign = jnp.where((lane & (ROPE_D // 2)) == 0, -1.0, 1.0)
    cos_ref[...] = jnp.cos(ang)
    sin_ref[...] = jnp.sin(ang) * sign


def _rope_tables(positions):
    n = positions.size
    tm = 1024
    inv = ROPE_THETA ** (-jnp.arange(0, ROPE_D, 2, dtype=F32) / ROPE_D)
    inv_t = jnp.tile(inv, LANES // (ROPE_D // 2))[None, :]
    pos = positions.reshape(n, 1)
    return pl.pallas_call(
        _rope_table_kernel,
        grid=(n // tm,),
        in_specs=[pl.BlockSpec((tm, 1), lambda i: (i, 0)),
                  pl.BlockSpec((1, LANES), lambda i: (0, 0))],
        out_specs=[pl.BlockSpec((tm, LANES), lambda i: (i, 0)),
                   pl.BlockSpec((tm, LANES), lambda i: (i, 0))],
        out_shape=[jax.ShapeDtypeStruct((n, LANES), F32)] * 2,
        name="rope_tables",
    )(pos, inv_t)


def _mem_kv_kernel(mem_ref, g_ref, w_ref, kn_ref, bd_ref, k_ref, v_ref):
    m = mem_ref[0]
    mn = (_rms(m) * g_ref[0]).astype(BF16)
    kv = jnp.dot(mn, w_ref[0], preferred_element_type=F32)
    k = kv[:, :MEM_W]
    k = k * lax.rsqrt(_seg_sum_sq(k, bd_ref) * (1.0 / MEM_D) + RMS_EPS) * kn_ref[0]
    k_ref[0, 0] = k.astype(BF16)
    v_ref[0, 0] = kv[:, MEM_W:].astype(BF16)


def _mem_kv(mem, mem_norm_g, w_mem_kv, mem_kn_g, bd_mem):
    depth = w_mem_kv.shape[0]
    b, m, d = mem.shape
    kn_t = jnp.tile(mem_kn_g, (1, MEM_HEADS))[:, None, :]
    return pl.pallas_call(
        _mem_kv_kernel,
        grid=(depth, b),
        in_specs=[pl.BlockSpec((1, m, d), lambda l, i: (i, 0, 0)),
                  pl.BlockSpec((1, 1, d), lambda l, i: (l, 0, 0)),
                  pl.BlockSpec((1, d, 2 * MEM_W), lambda l, i: (l, 0, 0)),
                  pl.BlockSpec((1, 1, MEM_W), lambda l, i: (l, 0, 0)),
                  pl.BlockSpec(bd_mem.shape, lambda l, i: (0, 0))],
        out_specs=[pl.BlockSpec((1, 1, m, MEM_W), lambda l, i: (l, i, 0, 0)),
                   pl.BlockSpec((1, 1, m, MEM_W), lambda l, i: (l, i, 0, 0))],
        out_shape=[jax.ShapeDtypeStruct((depth, b, m, MEM_W), BF16)] * 2,
        name="mem_kv",
    )(mem, mem_norm_g[:, None, :], w_mem_kv.astype(BF16), kn_t, bd_mem)


_PROJ_PARAMS = ("gx", "w_in", "gcq", "gckv", "w_uq", "w_uq_sw", "w_uk", "w_uv_t",
                "q_gc1", "q_gc0", "q_gs", "k_gc1", "k_gc0", "k_gs",
                "dq_gc", "dq_gs", "dk_gc", "dk_gs", "mqn", "bd_diff", "bd_mem")


def _proj_kernel(x_ref, cos_ref, sin_ref, gx_ref, win_ref, gcq_ref, gckv_ref, wuq_ref, wuqsw_ref, wuk_ref, wuvt_ref,
                 qgc1_ref, qgc0_ref, qgs_ref, kgc1_ref, kgc0_ref, kgs_ref,
                 dqgc_ref, dqgs_ref, dkgc_ref, dkgs_ref, mqn_ref, bdd_ref, bdm_ref, kmem_ref, vmem_ref,
                 q_out, k_out, v_out, dq_out, dk_out, dv_out, ymem_out, gate_out):
    tm = x_ref.shape[0]
    h = (_rms(x_ref[...]) * gx_ref[...]).astype(BF16)
    proj = jnp.dot(h, win_ref[...], preferred_element_type=F32)
    cos = cos_ref[...]
    sin = sin_ref[...]

    cqn = (_rms(proj[:, _C_CQ:_C_CQ + Q_LORA]) * gcq_ref[...]).astype(BF16)
    q = jnp.dot(cqn, wuq_ref[...], preferred_element_type=F32)
    q_sw = jnp.dot(cqn, wuqsw_ref[...], preferred_element_type=F32)
    q_gc = cos * qgc1_ref[...] + qgc0_ref[...]
    q_gs = sin * qgs_ref[...]
    for hd in range(MLA_HEADS):
        sl = slice(hd * HEAD_PAD, (hd + 1) * HEAD_PAD)
        qh = q[:, sl]
        r = lax.rsqrt(jnp.sum(qh * qh, axis=-1, keepdims=True) * (1.0 / MLA_QK) + RMS_EPS)
        q_out[:, sl] = ((qh * q_gc + q_sw[:, sl] * q_gs) * r).astype(BF16)

    ckv = _rms(proj[:, _C_CKV:_C_CKV + KV_LORA]) * gckv_ref[...]
    ckvn = ckv.astype(BF16)
    k_nope = jnp.dot(ckvn, wuk_ref[...], preferred_element_type=F32)
    kr = proj[:, _C_KR:_C_KR + LANES]
    k_gc = cos * kgc1_ref[...] + kgc0_ref[...]
    kr_rot = proj[:, _C_KR_SW:_C_KR_SW + LANES] * (sin * kgs_ref[...])
    for hd in range(MLA_HEADS):
        sl = slice(hd * HEAD_PAD, (hd + 1) * HEAD_PAD)
        kh = k_nope[:, sl] + kr
        r = lax.rsqrt(jnp.sum(kh * kh, axis=-1, keepdims=True) * (1.0 / MLA_QK) + RMS_EPS)
        k_out[:, sl] = ((kh * k_gc + kr_rot) * r).astype(BF16)
    vt = jnp.dot(wuvt_ref[...], ckv.T.astype(BF16), preferred_element_type=F32)
    ones_row = (lax.broadcasted_iota(jnp.int32, (VT_ROWS, tm), 0) == V_DIM).astype(F32)
    for hd in range(MLA_HEADS):
        rows = slice(hd * VT_ROWS, (hd + 1) * VT_ROWS)
        v_out[0, rows, :] = (vt[rows] + ones_row).astype(BF16)

    for raw_c, sw_c, gc_ref, gs_ref, out in ((_C_DQ, _C_DQ_SW, dqgc_ref, dqgs_ref, dq_out),
                                             (_C_DK, _C_DK_SW, dkgc_ref, dkgs_ref, dk_out)):
        raw = proj[:, raw_c:raw_c + DIFF_QK_W]
        r = lax.rsqrt(_seg_sum_sq(raw, bdd_ref) * (1.0 / DIFF_D) + RMS_EPS)
        for c in range(DIFF_QK_W // LANES):
            sl = slice(c * LANES, (c + 1) * LANES)
            rot = raw[:, sl] * (cos * gc_ref[:, sl]) + proj[:, sw_c + c * LANES:sw_c + (c + 1) * LANES] * (sin * gs_ref[:, sl])
            out[:, sl] = (rot * r[:, sl]).astype(BF16)
    ones_lane = (_lane_iota((1, LANES)) == V_DIM).astype(F32)
    for hd in range(DIFF_HEADS):
        dv_t = (proj[:, _C_DV + hd * LANES:_C_DV + (hd + 1) * LANES] + ones_lane).T
        dv_out[0, hd * VT_ROWS:(hd + 1) * VT_ROWS, :] = dv_t[:VT_ROWS].astype(BF16)

    z = proj[:, _C_Z:_C_END]
    gate = z * (1.0 / (1.0 + jnp.exp(-z)))
    gate_out[...] = gate[:, :MLA_W + DIFF_W].astype(BF16)

    mq = proj[:, _C_MQ:_C_MQ + MEM_W]
    mq = mq * lax.rsqrt(_seg_sum_sq(mq, bdm_ref) * (1.0 / MEM_D) + RMS_EPS) * mqn_ref[...]
    kmem = kmem_ref[0]
    vmem = vmem_ref[0]
    lane_w = _lane_iota((1, MEM_W))
    o = jnp.zeros(mq.shape, F32)
    for hd in range(MEM_HEADS):
        head_mask = (lane_w // MEM_D) == hd
        qh = jnp.where(head_mask, mq, 0.0).astype(BF16)
        s = lax.dot_general(qh, kmem, _NT, preferred_element_type=F32)
        p = jnp.exp(s - jnp.max(s, axis=-1, keepdims=True))
        inv_l = 1.0 / jnp.sum(p, axis=-1, keepdims=True)
        vh = jnp.where(head_mask, vmem, jnp.zeros_like(vmem))
        o = o + jnp.dot(p.astype(BF16), vh, preferred_element_type=F32) * inv_l
    ymem_out[...] = (o * gate[:, MLA_W + DIFF_W:]).astype(BF16)


def _proj_call(x2d, cos_t, sin_t, kmem, vmem, p, seq):
    n = x2d.shape[0]
    tm = TM_PROJ
    per_batch = seq // tm
    row = lambda i: (i, 0)
    const = lambda i: (0, 0)
    params = [p[name] for name in _PROJ_PARAMS]
    mem_spec = pl.BlockSpec((1,) + kmem.shape[1:], lambda i: (i // per_batch, 0, 0))
    in_specs = ([pl.BlockSpec((tm, D_MODEL), row), pl.BlockSpec((tm, LANES), row), pl.BlockSpec((tm, LANES), row)]
                + [pl.BlockSpec(a.shape, const) for a in params] + [mem_spec, mem_spec])

    def rows_out(width):
        return pl.BlockSpec((tm, width), row), jax.ShapeDtypeStruct((n, width), BF16)

    def values_t_out(heads):
        return (pl.BlockSpec((1, heads * VT_ROWS, tm), lambda i: (i, 0, 0)),
                jax.ShapeDtypeStruct((n // tm, heads * VT_ROWS, tm), BF16))

    outs = [rows_out(MLA_HEADS * HEAD_PAD), rows_out(MLA_HEADS * HEAD_PAD), values_t_out(MLA_HEADS),
            rows_out(DIFF_QK_W), rows_out(DIFF_QK_W), values_t_out(DIFF_HEADS),
            rows_out(MEM_W), rows_out(MLA_W + DIFF_W)]
    return pl.pallas_call(
        _proj_kernel,
        grid=(n // tm,),
        in_specs=in_specs,
        out_specs=[o[0] for o in outs],
        out_shape=[o[1] for o in outs],
        compiler_params=pltpu.CompilerParams(dimension_semantics=("arbitrary",), vmem_limit_bytes=VMEM_LIMIT),
        name="proj_prep",
    )(x2d, cos_t, sin_t, *params, kmem, vmem)


def _causal_keep_t(tk, tq, key_offset):
    r = lax.broadcasted_iota(jnp.int32, (tk, tq), 0)
    c = lax.broadcasted_iota(jnp.int32, (tk, tq), 1)
    return (r + key_offset) <= c


def _init_limits(lim_ref, tk):
    for c in range(lim_ref.shape[1]):
        lim_ref[0, c] = jnp.full((tk, MXU_DIM), SCORE_CEIL, F32)
        lim_ref[1, c] = jnp.where(_causal_keep_t(tk, MXU_DIM, -c * MXU_DIM), SCORE_CEIL, NEG_INF)


def _score_strip(k_blk, q_strip, lim_ref, diag, slot, idx, c):
    s_ref, mb_ref = slot
    s = lax.dot_general(k_blk, q_strip, _NT, preferred_element_type=F32)
    s = jnp.minimum(s, lim_ref[diag, c])
    s_ref[idx, c] = s
    mb_ref[idx, c] = jnp.max(s, axis=0, keepdims=True)


def _softmax_pv_strip(slot, idx, c, vt_blk, first, m_ref, acc_ref, qi):
    s_ref, mb_ref = slot
    s = s_ref[idx, c]
    m_old = jnp.where(first, NEG_INF, m_ref[idx, c])
    m_new = jnp.maximum(m_old, mb_ref[idx, c])
    alpha = jnp.exp2(m_old - m_new)
    p = jnp.exp2(s - m_new).astype(BF16)
    acc_ref[qi, idx, c] = acc_ref[qi, idx, c] * alpha + jnp.dot(vt_blk, p, preferred_element_type=F32)
    m_ref[idx, c] = m_new


def _causal_steps(nq):
    qi = np.concatenate([np.full(i + 1, i, np.int32) for i in range(nq)])
    kj = np.concatenate([np.arange(i + 1, dtype=np.int32) for i in range(nq)])
    return qi, kj


def _flat_sweep(n_steps, scores, update):
    def pair(t0, u, with_scores):
        ahead = scores(t0 + u + 1, (u + 1) % SCORE_SLOTS) if with_scores else []
        now = update(t0 + u, u % SCORE_SLOTS)
        for i in range(max(len(ahead), len(now))):
            if i < len(ahead):
                ahead[i]()
            if i < len(now):
                now[i]()

    for piece in scores(0, 0):
        piece()
    trips = (n_steps - 1) // TRIP_STEPS

    def body(i, carry):
        for u in range(TRIP_STEPS):
            pair(i * TRIP_STEPS, u, True)
        return carry

    lax.fori_loop(0, trips, body, 0)
    done = trips * TRIP_STEPS
    for u in range(n_steps - done):
        pair(done, u, done + u + 1 < n_steps)


def _normalised_t(acc_ref, qb, idx):
    strips = []
    for c in range(acc_ref.shape[2]):
        acc = acc_ref[qb, idx, c]
        strips.append(acc[:V_DIM] * (1.0 / acc[V_DIM:V_DIM + 1]))
    return jnp.concatenate(strips, axis=1)


def _attn_scratch(count, nq, tk, tq):
    ns = tq // MXU_DIM
    shapes = []
    for _ in range(SCORE_SLOTS):
        shapes += [pltpu.VMEM((count, ns, tk, MXU_DIM), F32), pltpu.VMEM((count, ns, 1, MXU_DIM), F32)]
    return shapes + [pltpu.VMEM((2, ns, tk, MXU_DIM), F32), pltpu.VMEM((count, ns, 1, MXU_DIM), F32),
                     pltpu.VMEM((nq, count, ns, VT_ROWS, MXU_DIM), F32)]


def _split_scratch(scratch):
    slots = [(scratch[2 * i], scratch[2 * i + 1]) for i in range(SCORE_SLOTS)]
    lim_ref, m_ref, acc_ref = scratch[2 * SCORE_SLOTS:]
    return slots, lim_ref, m_ref, acc_ref


def _mla_attn_kernel(qi_tab, kj_tab, q0_ref, q1_ref, k0_ref, k1_ref, vt_ref, gate_ref, o_ref, *scratch, tq, tk, nq):
    slots, lim_ref, m_ref, acc_ref = _split_scratch(scratch)
    q_refs, k_refs = (q0_ref, q1_ref), (k0_ref, k1_ref)
    n_strips = tq // MXU_DIM
    _init_limits(lim_ref, tk)
    acc_ref[...] = jnp.zeros(acc_ref.shape, F32)

    def scores(t, slot):
        qi, kj = qi_tab[t], kj_tab[t]
        krows = pl.ds(pl.multiple_of(kj * tk, tk), tk)
        diag = (kj == qi).astype(jnp.int32)

        def piece(hd, c):
            qrows = pl.ds(pl.multiple_of(qi * tq + c * MXU_DIM, MXU_DIM), MXU_DIM)
            _score_strip(k_refs[hd][krows, :], q_refs[hd][qrows, :], lim_ref, diag, slots[slot], hd, c)

        return [functools.partial(piece, hd, c) for hd in range(2) for c in range(n_strips)]

    def update(t, slot):
        qi, kj = qi_tab[t], kj_tab[t]

        def piece(hd, c):
            _softmax_pv_strip(slots[slot], hd, c, vt_ref[kj, hd * VT_ROWS:(hd + 1) * VT_ROWS, :], kj == 0,
                              m_ref, acc_ref, qi)

        return [functools.partial(piece, hd, c) for hd in range(2) for c in range(n_strips)]

    _flat_sweep(nq * (nq + 1) // 2, scores, update)

    def finish(qb, carry):
        rows = pl.ds(pl.multiple_of(qb * tq, tq), tq)
        o_t = jnp.concatenate([_normalised_t(acc_ref, qb, 0), _normalised_t(acc_ref, qb, 1)], axis=0)
        o_ref[rows, :] = (o_t.T * gate_ref[rows, :].astype(F32)).astype(BF16)
        return carry

    lax.fori_loop(0, nq, finish, 0)


def _mla_attn_call(q, k, vt, gate, batch, seq):
    tq, tk = TQ_MLA, TK
    nq = seq // tq
    pairs = MLA_HEADS // 2
    qi_tab, kj_tab = _causal_steps(nq)
    blk = lambda b, hp, *_: (b, hp)
    head0 = pl.BlockSpec((seq, HEAD_PAD), lambda b, hp, *_: (b, 2 * hp))
    head1 = pl.BlockSpec((seq, HEAD_PAD), lambda b, hp, *_: (b, 2 * hp + 1))
    grid_spec = pltpu.PrefetchScalarGridSpec(
        num_scalar_prefetch=2,
        grid=(batch, pairs),
        in_specs=[head0, head1, head0, head1,
                  pl.BlockSpec((seq // tk, 2 * VT_ROWS, tk), lambda b, hp, *_: (b, hp, 0)),
                  pl.BlockSpec((seq, LANES), blk)],
        out_specs=pl.BlockSpec((seq, LANES), blk),
        scratch_shapes=_attn_scratch(2, nq, tk, tq))
    return pl.pallas_call(
        functools.partial(_mla_attn_kernel, tq=tq, tk=tk, nq=nq),
        grid_spec=grid_spec,
        out_shape=jax.ShapeDtypeStruct((batch * seq, MLA_W), BF16),
        compiler_params=pltpu.CompilerParams(dimension_semantics=("arbitrary",) * 2, vmem_limit_bytes=VMEM_LIMIT),
        name="mla_attn",
    )(qi_tab, kj_tab, q, q, k, k, vt, gate)


def _diff_attn_kernel(qi_tab, kj_tab, q_ref, k_ref, vt_ref, gate_ref, lam_ref, sg_ref, o_ref, *scratch,
                      tq, tk, nq, lam_init):
    slots, lim_ref, m_ref, acc_ref = _split_scratch(scratch)
    n_strips = tq // MXU_DIM
    _init_limits(lim_ref, tk)
    acc_ref[...] = jnp.zeros(acc_ref.shape, F32)
    map_of_lane = _lane_iota((1, LANES)) // DIFF_D

    def scores(t, slot):
        qi, kj = qi_tab[t], kj_tab[t]
        krows = pl.ds(pl.multiple_of(kj * tk, tk), tk)
        diag = (kj == qi).astype(jnp.int32)

        def piece(idx, c):
            q_strip = q_ref[pl.ds(pl.multiple_of(qi * tq + c * MXU_DIM, MXU_DIM), MXU_DIM), :]
            q_map = jnp.where(map_of_lane == idx, q_strip, jnp.zeros_like(q_strip))
            _score_strip(k_ref[krows, :], q_map, lim_ref, diag, slots[slot], idx, c)

        return [functools.partial(piece, idx, c) for idx in range(4) for c in range(n_strips)]

    def update(t, slot):
        qi, kj = qi_tab[t], kj_tab[t]

        def piece(idx, c):
            hd = idx // 2
            _softmax_pv_strip(slots[slot], idx, c, vt_ref[kj, hd * VT_ROWS:(hd + 1) * VT_ROWS, :], kj == 0,
                              m_ref, acc_ref, qi)

        return [functools.partial(piece, idx, c) for idx in range(4) for c in range(n_strips)]

    _flat_sweep(nq * (nq + 1) // 2, scores, update)

    lv = lam_ref[...]
    lam = (jnp.exp(jnp.sum(lv[0:1] * lv[1:2], axis=-1, keepdims=True))
           - jnp.exp(jnp.sum(lv[2:3] * lv[3:4], axis=-1, keepdims=True)) + lam_init)

    def finish(qb, carry):
        rows = pl.ds(pl.multiple_of(qb * tq, tq), tq)
        heads_t = []
        for hd in range(2):
            o = _normalised_t(acc_ref, qb, 2 * hd) - lam * _normalised_t(acc_ref, qb, 2 * hd + 1)
            heads_t.append(o * lax.rsqrt(jnp.mean(o * o, axis=0, keepdims=True) + RMS_EPS))
        o = jnp.concatenate(heads_t, axis=0).T * sg_ref[...] * (1.0 - lam_init)
        o_ref[rows, :] = (o * gate_ref[rows, :].astype(F32)).astype(BF16)
        return carry

    lax.fori_loop(0, nq, finish, 0)


def _diff_attn_call(q, k, vt, gate, lam_pad, subln_t, batch, seq, lam_init):
    tq, tk = TQ_DIFF, TK
    nq = seq // tq
    pairs = DIFF_HEADS // 2
    gate_off = MLA_W // LANES
    qi_tab, kj_tab = _causal_steps(nq)
    blk = lambda b, hp, *_: (b, hp)
    const = lambda b, hp, *_: (0, 0)
    grid_spec = pltpu.PrefetchScalarGridSpec(
        num_scalar_prefetch=2,
        grid=(batch, pairs),
        in_specs=[pl.BlockSpec((seq, LANES), blk),
                  pl.BlockSpec((seq, LANES), blk),
                  pl.BlockSpec((seq // tk, 2 * VT_ROWS, tk), lambda b, hp, *_: (b, hp, 0)),
                  pl.BlockSpec((seq, LANES), lambda b, hp, *_: (b, gate_off + hp)),
                  pl.BlockSpec(lam_pad.shape, const),
                  pl.BlockSpec(subln_t.shape, const)],
        out_specs=pl.BlockSpec((seq, LANES), blk),
        scratch_shapes=_attn_scratch(4, nq, tk, tq))
    return pl.pallas_call(
        functools.partial(_diff_attn_kernel, tq=tq, tk=tk, nq=nq, lam_init=lam_init),
        grid_spec=grid_spec,
        out_shape=jax.ShapeDtypeStruct((batch * seq, DIFF_W), BF16),
        compiler_params=pltpu.CompilerParams(dimension_semantics=("arbitrary",) * 2, vmem_limit_bytes=VMEM_LIMIT),
        name="diff_attn",
    )(qi_tab, kj_tab, q, k, vt, gate, lam_pad, subln_t)


def _out_kernel(x_ref, ymla_ref, ydiff_ref, ymem_ref, w_ref, o_ref):
    y = jnp.concatenate([ymla_ref[...], ydiff_ref[...], ymem_ref[...]], axis=1)
    o_ref[...] = x_ref[...] + jnp.dot(y, w_ref[...], preferred_element_type=F32)


def _out_call(x2d, y_mla, y_diff, y_mem, w_out):
    n = x2d.shape[0]
    tm = TM_OUT
    row = lambda i: (i, 0)
    return pl.pallas_call(
        _out_kernel,
        grid=(n // tm,),
        in_specs=[pl.BlockSpec((tm, D_MODEL), row), pl.BlockSpec((tm, MLA_W), row),
                  pl.BlockSpec((tm, DIFF_W), row), pl.BlockSpec((tm, MEM_W), row),
                  pl.BlockSpec(w_out.shape, lambda i: (0, 0))],
        out_specs=pl.BlockSpec((tm, D_MODEL), row),
        out_shape=jax.ShapeDtypeStruct((n, D_MODEL), F32),
        compiler_params=pltpu.CompilerParams(dimension_semantics=("arbitrary",), vmem_limit_bytes=VMEM_LIMIT),
        name="out_proj",
    )(x2d, y_mla, y_diff, y_mem, w_out)


def _block_diag_ones(seg):
    idx = np.arange(MXU_DIM) // seg
    return jnp.asarray(idx[:, None] == idx[None, :], dtype=BF16)


def _pair_swapped(a):
    lead = a.shape[:-1]
    a4 = a.reshape(lead + (a.shape[-1] // ROPE_D, 2, ROPE_D // 2))
    return a4[..., ::-1, :].reshape(a.shape)


def _head_padded(a, heads, width, offset=0):
    rows = a.shape[0]
    a3 = a.reshape(rows, heads, width)
    return jnp.pad(a3, ((0, 0), (0, 0), (offset, LANES - width - offset))).reshape(rows, heads * LANES)


def _rope_lane_vec(g_rope):
    return jnp.pad(g_rope, (MLA_NOPE, LANES - MLA_QK))[None, :]


def _layer_params(l, norm_g, w_in, mla_q_norm_g, mla_kv_norm_g, w_uq, w_ukv, mla_qn_g, mla_kn_g,
                  diff_qn_g, diff_kn_g, mem_qn_g):
    w = w_in[l]
    o_cq, o_ckv, o_kr = 0, Q_LORA, Q_LORA + KV_LORA
    o_dq = o_kr + MLA_ROPE
    o_dk = o_dq + DIFF_QK_W
    o_dv = o_dk + DIFF_QK_W
    o_mq = o_dv + DIFF_W
    o_z = o_mq + MEM_W
    w_kr, w_dq, w_dk = w[:, o_kr:o_dq], w[:, o_dq:o_dk], w[:, o_dk:o_dv]
    w_in_p = jnp.concatenate([
        w[:, o_cq:o_ckv], w[:, o_ckv:o_kr],
        _head_padded(w_kr, 1, MLA_ROPE, MLA_NOPE), _head_padded(_pair_swapped(w_kr), 1, MLA_ROPE, MLA_NOPE),
        w_dq, _pair_swapped(w_dq), w_dk, _pair_swapped(w_dk),
        _head_padded(w[:, o_dv:o_mq], DIFF_HEADS, DIFF_V),
        w[:, o_mq:o_z], w[:, o_z:]], axis=1).astype(BF16)

    uq3 = w_uq[l].reshape(Q_LORA, MLA_HEADS, MLA_QK)
    w_uq_p = _head_padded(w_uq[l], MLA_HEADS, MLA_QK).astype(BF16)
    w_uq_sw = _head_padded(_pair_swapped(uq3[:, :, MLA_NOPE:]).reshape(Q_LORA, MLA_HEADS * MLA_ROPE),
                           MLA_HEADS, MLA_ROPE, MLA_NOPE).astype(BF16)
    kv3 = w_ukv[l].reshape(KV_LORA, MLA_HEADS, MLA_NOPE + MLA_V)
    w_uk_p = _head_padded(kv3[:, :, :MLA_NOPE].reshape(KV_LORA, MLA_HEADS * MLA_NOPE), MLA_HEADS, MLA_NOPE).astype(BF16)
    w_uv_t = jnp.pad(jnp.transpose(kv3[:, :, MLA_NOPE:], (1, 2, 0)),
                     ((0, 0), (0, VT_ROWS - MLA_V), (0, 0))).reshape(MLA_HEADS * VT_ROWS, KV_LORA).astype(BF16)

    q_scale = LOG2_E / math.sqrt(MLA_QK)
    d_scale = LOG2_E / math.sqrt(DIFF_D)
    qn, kn = mla_qn_g[l], mla_kn_g[l]
    nope_vec = lambda g: jnp.pad(g[:MLA_NOPE], (0, LANES - MLA_NOPE))[None, :]
    dqn = jnp.tile(diff_qn_g[l], 2 * DIFF_HEADS)[None, :]
    dkn = jnp.tile(diff_kn_g[l], 2 * DIFF_HEADS)[None, :]
    return {
        "gx": norm_g[l][None, :],
        "w_in": w_in_p,
        "gcq": mla_q_norm_g[l][None, :],
        "gckv": mla_kv_norm_g[l][None, :],
        "w_uq": w_uq_p, "w_uq_sw": w_uq_sw, "w_uk": w_uk_p, "w_uv_t": w_uv_t,
        "q_gc1": _rope_lane_vec(qn[MLA_NOPE:]) * q_scale, "q_gc0": nope_vec(qn) * q_scale,
        "q_gs": _rope_lane_vec(_pair_swapped(qn[MLA_NOPE:])) * q_scale,
        "k_gc1": _rope_lane_vec(kn[MLA_NOPE:]), "k_gc0": nope_vec(kn),
        "k_gs": _rope_lane_vec(_pair_swapped(kn[MLA_NOPE:])),
        "dq_gc": dqn * d_scale, "dq_gs": _pair_swapped(dqn) * d_scale,
        "dk_gc": dkn, "dk_gs": _pair_swapped(dkn),
        "mqn": jnp.tile(mem_qn_g[l], MEM_HEADS)[None, :] * (1.0 / math.sqrt(MEM_D)),
        "bd_diff": _block_diag_ones(DIFF_D), "bd_mem": _block_diag_ones(MEM_D),
    }


def kernel(x, mem, positions, norm_g, w_in, mla_q_norm_g, mla_kv_norm_g, w_uq, w_ukv, mla_qn_g, mla_kn_g,
           diff_qn_g, diff_kn_g, diff_lambda, diff_subln_g, mem_norm_g, w_mem_kv, mem_qn_g, mem_kn_g, w_out):
    batch, seq, d_model = x.shape
    depth = w_in.shape[0]
    assert d_model == D_MODEL and MLA_V == DIFF_V and TQ_MLA == TK and TQ_DIFF == TK
    assert TRIP_STEPS % SCORE_SLOTS == 0 and SCORE_SLOTS >= 2 and TK % MXU_DIM == 0
    assert seq % max(TM_PROJ, TQ_MLA, TQ_DIFF, TM_OUT) == 0 and (batch * seq) % 1024 == 0
    cos_t, sin_t = _rope_tables(positions)
    kmem_all, vmem_all = _mem_kv(mem, mem_norm_g, w_mem_kv, mem_kn_g, _block_diag_ones(MEM_D))
    x2d = x.reshape(batch * seq, D_MODEL)
    for l in range(depth):
        lam_init = 0.8 - 0.6 * math.exp(-0.3 * l)
        p = _layer_params(l, norm_g, w_in, mla_q_norm_g, mla_kv_norm_g, w_uq, w_ukv, mla_qn_g, mla_kn_g,
                          diff_qn_g, diff_kn_g, mem_qn_g)
        q, k, v, dq, dk, dv, y_mem, gate = _proj_call(x2d, cos_t, sin_t, kmem_all[l], vmem_all[l], p, seq)
        y_mla = _mla_attn_call(q, k, v, gate, batch, seq)
        lam_pad = jnp.pad(diff_lambda[l], ((0, 4), (0, LANES - DIFF_D)))
        subln_t = jnp.tile(diff_subln_g[l], 2)[None, :]
        y_diff = _diff_attn_call(dq, dk, dv, gate, lam_pad, subln_t, batch, seq, lam_init)
        x2d = _out_call(x2d, y_mla, y_diff, y_mem, w_out[l].astype(BF16))
    return x2d.reshape(batch, seq, D_MODEL)
```

```python
import functools
import math

import numpy as np
import jax
import jax.numpy as jnp
from jax import lax
from jax.experimental import pallas as pl
from jax.experimental.pallas import tpu as pltpu

F32 = jnp.float32
BF16 = jnp.bfloat16

D_MODEL = 1024
ROPE_THETA = 10000.0
RMS_EPS = 1e-6
NEG_INF = -1e30
LOG2_E = math.log2(math.e)

MLA_HEADS = 8
Q_LORA = 256
KV_LORA = 128
MLA_NOPE = 64
MLA_ROPE = 32
MLA_QK = MLA_NOPE + MLA_ROPE
MLA_V = 64
MLA_W = MLA_HEADS * MLA_V

DIFF_HEADS = 4
DIFF_D = 32
DIFF_V = 2 * DIFF_D
DIFF_W = DIFF_HEADS * DIFF_V
DIFF_QK_W = 2 * DIFF_HEADS * DIFF_D

MEM_HEADS = 4
MEM_D = 64
MEM_W = MEM_HEADS * MEM_D

D_MIX = MLA_W + DIFF_W + MEM_W

LANES = 128
MXU_DIM = 256
ROPE_D = 32
HEAD_PAD = LANES

_C_CQ = 0
_C_CKV = _C_CQ + Q_LORA
_C_KR = _C_CKV + KV_LORA
_C_KR_SW = _C_KR + LANES
_C_DQ = _C_KR_SW + LANES
_C_DQ_SW = _C_DQ + DIFF_QK_W
_C_DK = _C_DQ_SW + DIFF_QK_W
_C_DK_SW = _C_DK + DIFF_QK_W
_C_DV = _C_DK_SW + DIFF_QK_W
_C_MQ = _C_DV + DIFF_HEADS * LANES
_C_Z = _C_MQ + MEM_W
_C_END = _C_Z + D_MIX

VMEM_LIMIT = 56 * 1024 * 1024

V_DIM = MLA_V
VT_ROWS = 80

TK = 512
TM_PROJ = TK
TM_OUT = 512
TQ_MLA = 512
TQ_DIFF = 512
SCORE_SLOTS = 2
TRIP_STEPS = 4
SCORE_CEIL = 3.0e38

_NT = (((1,), (1,)), ((), ()))


def _rms(x):
    return x * lax.rsqrt(jnp.mean(x * x, axis=-1, keepdims=True) + RMS_EPS)


def _lane_iota(shape):
    return lax.broadcasted_iota(jnp.int32, shape, len(shape) - 1)


def _seg_sum_sq(x, bd_ref):
    sq = x * x
    hi = sq.astype(BF16)
    lo = (sq - hi.astype(F32)).astype(BF16)
    bd = bd_ref[...]
    return jnp.dot(hi, bd, preferred_element_type=F32) + jnp.dot(lo, bd, preferred_element_type=F32)


def _rope_table_kernel(pos_ref, inv_ref, cos_ref, sin_ref):
    ang = pos_ref[...].astype(F32) * inv_ref[...]
    lane = _lane_iota(ang.shape)
    sign = jnp.where((lane & (ROPE_D // 2)) == 0, -1.0, 1.0)
    cos_ref[...] = jnp.cos(ang)
    sin_ref[...] = jnp.sin(ang) * sign


def _rope_tables(positions):
    n = positions.size
    tm = 1024
    inv = ROPE_THETA ** (-jnp.arange(0, ROPE_D, 2, dtype=F32) / ROPE_D)
    inv_t = jnp.tile(inv, LANES // (ROPE_D // 2))[None, :]
    pos = positions.reshape(n, 1)
    return pl.pallas_call(
        _rope_table_kernel,
        grid=(n // tm,),
        in_specs=[pl.BlockSpec((tm, 1), lambda i: (i, 0)),
                  pl.BlockSpec((1, LANES), lambda i: (0, 0))],
        out_specs=[pl.BlockSpec((tm, LANES), lambda i: (i, 0)),
                   pl.BlockSpec((tm, LANES), lambda i: (i, 0))],
        out_shape=[jax.ShapeDtypeStruct((n, LANES), F32)] * 2,
        name="rope_tables",
    )(pos, inv_t)


def _mem_kv_kernel(mem_ref, g_ref, w_ref, kn_ref, bd_ref, k_ref, v_ref):
    m = mem_ref[0]
    mn = (_rms(m) * g_ref[0]).astype(BF16)
    kv = jnp.dot(mn, w_ref[0], preferred_element_type=F32)
    k = kv[:, :MEM_W]
    k = k * lax.rsqrt(_seg_sum_sq(k, bd_ref) * (1.0 / MEM_D) + RMS_EPS) * kn_ref[0]
    k_ref[0, 0] = k.astype(BF16)
    v_ref[0, 0] = kv[:, MEM_W:].astype(BF16)


def _mem_kv(mem, mem_norm_g, w_mem_kv, mem_kn_g, bd_mem):
    depth = w_mem_kv.shape[0]
    b, m, d = mem.shape
    kn_t = jnp.tile(mem_kn_g, (1, MEM_HEADS))[:, None, :]
    return pl.pallas_call(
        _mem_kv_kernel,
        grid=(depth, b),
        in_specs=[pl.BlockSpec((1, m, d), lambda l, i: (i, 0, 0)),
                  pl.BlockSpec((1, 1, d), lambda l, i: (l, 0, 0)),
                  pl.BlockSpec((1, d, 2 * MEM_W), lambda l, i: (l, 0, 0)),
                  pl.BlockSpec((1, 1, MEM_W), lambda l, i: (l, 0, 0)),
                  pl.BlockSpec(bd_mem.shape, lambda l, i: (0, 0))],
        out_specs=[pl.BlockSpec((1, 1, m, MEM_W), lambda l, i: (l, i, 0, 0)),
                   pl.BlockSpec((1, 1, m, MEM_W), lambda l, i: (l, i, 0, 0))],
        out_shape=[jax.ShapeDtypeStruct((depth, b, m, MEM_W), BF16)] * 2,
        name="mem_kv",
    )(mem, mem_norm_g[:, None, :], w_mem_kv.astype(BF16), kn_t, bd_mem)


_PROJ_PARAMS = ("gx", "w_in", "gcq", "gckv", "w_uq", "w_uq_sw", "w_uk", "w_uv_t",
                "q_gc1", "q_gc0", "q_gs", "k_gc1", "k_gc0", "k_gs",
                "dq_gc", "dq_gs", "dk_gc", "dk_gs", "mqn", "bd_diff", "bd_mem")


def _proj_kernel(x_ref, cos_ref, sin_ref, gx_ref, win_ref, gcq_ref, gckv_ref, wuq_ref, wuqsw_ref, wuk_ref, wuvt_ref,
                 qgc1_ref, qgc0_ref, qgs_ref, kgc1_ref, kgc0_ref, kgs_ref,
                 dqgc_ref, dqgs_ref, dkgc_ref, dkgs_ref, mqn_ref, bdd_ref, bdm_ref, kmem_ref, vmem_ref,
                 q_out, k_out, v_out, dq_out, dk_out, dv_out, ymem_out, gate_out):
    tm = x_ref.shape[0]
    h = (_rms(x_ref[...]) * gx_ref[...]).astype(BF16)
    proj = jnp.dot(h, win_ref[...], preferred_element_type=F32)
    cos = cos_ref[...]
    sin = sin_ref[...]

    cqn = (_rms(proj[:, _C_CQ:_C_CQ + Q_LORA]) * gcq_ref[...]).astype(BF16)
    q = jnp.dot(cqn, wuq_ref[...], preferred_element_type=F32)
    q_sw = jnp.dot(cqn, wuqsw_ref[...], preferred_element_type=F32)
    q_gc = cos * qgc1_ref[...] + qgc0_ref[...]
    q_gs = sin * qgs_ref[...]
    for hd in range(MLA_HEADS):
        sl = slice(hd * HEAD_PAD, (hd + 1) * HEAD_PAD)
        qh = q[:, sl]
        r = lax.rsqrt(jnp.sum(qh * qh, axis=-1, keepdims=True) * (1.0 / MLA_QK) + RMS_EPS)
        q_out[:, sl] = ((qh * q_gc + q_sw[:, sl] * q_gs) * r).astype(BF16)

    ckv = _rms(proj[:, _C_CKV:_C_CKV + KV_LORA]) * gckv_ref[...]
    ckvn = ckv.astype(BF16)
    k_nope = jnp.dot(ckvn, wuk_ref[...], preferred_element_type=F32)
    kr = proj[:, _C_KR:_C_KR + LANES]
    k_gc = cos * kgc1_ref[...] + kgc0_ref[...]
    kr_rot = proj[:, _C_KR_SW:_C_KR_SW + LANES] * (sin * kgs_ref[...])
    for hd in range(MLA_HEADS):
        sl = slice(hd * HEAD_PAD, (hd + 1) * HEAD_PAD)
        kh = k_nope[:, sl] + kr
        r = lax.rsqrt(jnp.sum(kh * kh, axis=-1, keepdims=True) * (1.0 / MLA_QK) + RMS_EPS)
        k_out[:, sl] = ((kh * k_gc + kr_rot) * r).astype(BF16)
    vt = jnp.dot(wuvt_ref[...], ckv.T.astype(BF16), preferred_element_type=F32)
    ones_row = (lax.broadcasted_iota(jnp.int32, (VT_ROWS, tm), 0) == V_DIM).astype(F32)
    for hd in range(MLA_HEADS):
        rows = slice(hd * VT_ROWS, (hd + 1) * VT_ROWS)
        v_out[0, rows, :] = (vt[rows] + ones_row).astype(BF16)

    for raw_c, sw_c, gc_ref, gs_ref, out in ((_C_DQ, _C_DQ_SW, dqgc_ref, dqgs_ref, dq_out),
                                             (_C_DK, _C_DK_SW, dkgc_ref, dkgs_ref, dk_out)):
        raw = proj[:, raw_c:raw_c + DIFF_QK_W]
        r = lax.rsqrt(_seg_sum_sq(raw, bdd_ref) * (1.0 / DIFF_D) + RMS_EPS)
        for c in range(DIFF_QK_W // LANES):
            sl = slice(c * LANES, (c + 1) * LANES)
            rot = raw[:, sl] * (cos * gc_ref[:, sl]) + proj[:, sw_c + c * LANES:sw_c + (c + 1) * LANES] * (sin * gs_ref[:, sl])
            out[:, sl] = (rot * r[:, sl]).astype(BF16)
    ones_lane = (_lane_iota((1, LANES)) == V_DIM).astype(F32)
    for hd in range(DIFF_HEADS):
        dv_t = (proj[:, _C_DV + hd * LANES:_C_DV + (hd + 1) * LANES] + ones_lane).T
        dv_out[0, hd * VT_ROWS:(hd + 1) * VT_ROWS, :] = dv_t[:VT_ROWS].astype(BF16)

    z = proj[:, _C_Z:_C_END]
    gate = z * (1.0 / (1.0 + jnp.exp(-z)))
    gate_out[...] = gate[:, :MLA_W + DIFF_W].astype(BF16)

    mq = proj[:, _C_MQ:_C_MQ + MEM_W]
    mq = mq * lax.rsqrt(_seg_sum_sq(mq, bdm_ref) * (1.0 / MEM_D) + RMS_EPS) * mqn_ref[...]
    kmem = kmem_ref[0]
    vmem = vmem_ref[0]
    lane_w = _lane_iota((1, MEM_W))
    o = jnp.zeros(mq.shape, F32)
    for hd in range(MEM_HEADS):
        head_mask = (lane_w // MEM_D) == hd
        qh = jnp.where(head_mask, mq, 0.0).astype(BF16)
        s = lax.dot_general(qh, kmem, _NT, preferred_element_type=F32)
        p = jnp.exp(s - jnp.max(s, axis=-1, keepdims=True))
        inv_l = 1.0 / jnp.sum(p, axis=-1, keepdims=True)
        vh = jnp.where(head_mask, vmem, jnp.zeros_like(vmem))
        o = o + jnp.dot(p.astype(BF16), vh, preferred_element_type=F32) * inv_l
    ymem_out[...] = (o * gate[:, MLA_W + DIFF_W:]).astype(BF16)


def _proj_call(x2d, cos_t, sin_t, kmem, vmem, p, seq):
    n = x2d.shape[0]
    tm = TM_PROJ
    per_batch = seq // tm
    row = lambda i: (i, 0)
    const = lambda i: (0, 0)
    params = [p[name] for name in _PROJ_PARAMS]
    mem_spec = pl.BlockSpec((1,) + kmem.shape[1:], lambda i: (i // per_batch, 0, 0))
    in_specs = ([pl.BlockSpec((tm, D_MODEL), row), pl.BlockSpec((tm, LANES), row), pl.BlockSpec((tm, LANES), row)]
                + [pl.BlockSpec(a.shape, const) for a in params] + [mem_spec, mem_spec])

    def rows_out(width):
        return pl.BlockSpec((tm, width), row), jax.ShapeDtypeStruct((n, width), BF16)

    def values_t_out(heads):
        return (pl.BlockSpec((1, heads * VT_ROWS, tm), lambda i: (i, 0, 0)),
                jax.ShapeDtypeStruct((n // tm, heads * VT_ROWS, tm), BF16))

    outs = [rows_out(MLA_HEADS * HEAD_PAD), rows_out(MLA_HEADS * HEAD_PAD), values_t_out(MLA_HEADS),
            rows_out(DIFF_QK_W), rows_out(DIFF_QK_W), values_t_out(DIFF_HEADS),
            rows_out(MEM_W), rows_out(MLA_W + DIFF_W)]
    return pl.pallas_call(
        _proj_kernel,
        grid=(n // tm,),
        in_specs=in_specs,
        out_specs=[o[0] for o in outs],
        out_shape=[o[1] for o in outs],
        compiler_params=pltpu.CompilerParams(dimension_semantics=("arbitrary",), vmem_limit_bytes=VMEM_LIMIT),
        name="proj_prep",
    )(x2d, cos_t, sin_t, *params, kmem, vmem)


def _causal_keep_t(tk, tq, key_offset):
    r = lax.broadcasted_iota(jnp.int32, (tk, tq), 0)
    c = lax.broadcasted_iota(jnp.int32, (tk, tq), 1)
    return (r + key_offset) <= c


def _init_limits(lim_ref, tk):
    for c in range(lim_ref.shape[1]):
        lim_ref[0, c] = jnp.full((tk, MXU_DIM), SCORE_CEIL, F32)
        lim_ref[1, c] = jnp.where(_causal_keep_t(tk, MXU_DIM, -c * MXU_DIM), SCORE_CEIL, NEG_INF)


def _score_strip(k_blk, q_strip, lim_ref, diag, slot, idx, c):
    s_ref, mb_ref = slot
    s = lax.dot_general(k_blk, q_strip, _NT, preferred_element_type=F32)
    if diag is not None:
        s = jnp.minimum(s, lim_ref[diag, c])
    s_ref[idx, c] = s
    mb_ref[idx, c] = jnp.max(s, axis=0, keepdims=True)


def _softmax_pv_strip(slot, idx, c, vt_blk, first, m_ref, acc_ref, qi):
    s_ref, mb_ref = slot
    s = s_ref[idx, c]
    m_old = jnp.where(first, NEG_INF, m_ref[idx, c])
    m_new = jnp.maximum(m_old, mb_ref[idx, c])
    alpha = jnp.exp2(m_old - m_new)
    p = jnp.exp2(s - m_new).astype(BF16)
    acc_ref[qi, idx, c] = acc_ref[qi, idx, c] * alpha + jnp.dot(vt_blk, p, preferred_element_type=F32)
    m_ref[idx, c] = m_new


def _causal_steps(nq):
    qi = np.concatenate([np.full(i + 1, i, np.int32) for i in range(nq)])
    kj = np.concatenate([np.arange(i + 1, dtype=np.int32) for i in range(nq)])
    trips = (len(qi) - 1) // TRIP_STEPS
    on_diag = (qi == kj)[1:1 + trips * TRIP_STEPS].reshape(trips, TRIP_STEPS)
    return qi, kj, np.append(on_diag.any(axis=1), False).astype(np.int32)


def _flat_sweep(n_steps, trip_diag, scores, update):
    def pair(t0, u, with_scores, masked):
        ahead = scores(t0 + u + 1, (u + 1) % SCORE_SLOTS, masked) if with_scores else []
        now = update(t0 + u, u % SCORE_SLOTS)
        for i in range(max(len(ahead), len(now))):
            if i < len(ahead):
                ahead[i]()
            if i < len(now):
                now[i]()

    for piece in scores(0, 0, True):
        piece()
    trips = (n_steps - 1) // TRIP_STEPS

    def trip(i, masked):
        for u in range(TRIP_STEPS):
            pair(i * TRIP_STEPS, u, True, masked)

    def body(i, carry):
        lax.cond(trip_diag[i] != 0, functools.partial(trip, i, True), functools.partial(trip, i, False))
        return carry

    lax.fori_loop(0, trips, body, 0)
    done = trips * TRIP_STEPS
    for u in range(n_steps - done):
        pair(done, u, done + u + 1 < n_steps, True)


def _normalised_t(acc_ref, qb, idx):
    strips = []
    for c in range(acc_ref.shape[2]):
        acc = acc_ref[qb, idx, c]
        strips.append(acc[:V_DIM] * (1.0 / acc[V_DIM:V_DIM + 1]))
    return jnp.concatenate(strips, axis=1)


def _attn_scratch(count, nq, tk, tq):
    ns = tq // MXU_DIM
    shapes = []
    for _ in range(SCORE_SLOTS):
        shapes += [pltpu.VMEM((count, ns, tk, MXU_DIM), F32), pltpu.VMEM((count, ns, 1, MXU_DIM), F32)]
    return shapes + [pltpu.VMEM((2, ns, tk, MXU_DIM), F32), pltpu.VMEM((count, ns, 1, MXU_DIM), F32),
                     pltpu.VMEM((nq, count, ns, VT_ROWS, MXU_DIM), F32)]


def _split_scratch(scratch, tk):
    slots = [(scratch[2 * i], scratch[2 * i + 1]) for i in range(SCORE_SLOTS)]
    lim_ref, m_ref, acc_ref = scratch[2 * SCORE_SLOTS:]
    _init_limits(lim_ref, tk)
    m_ref[...] = jnp.full(m_ref.shape, NEG_INF, F32)
    acc_ref[...] = jnp.zeros(acc_ref.shape, F32)
    return slots, lim_ref, m_ref, acc_ref


def _mla_attn_kernel(qi_tab, kj_tab, trip_diag, q0_ref, q1_ref, k0_ref, k1_ref, vt_ref, gate_ref, o_ref, *scratch,
                     tq, tk, nq):
    slots, lim_ref, m_ref, acc_ref = _split_scratch(scratch, tk)
    q_refs, k_refs = (q0_ref, q1_ref), (k0_ref, k1_ref)
    n_strips = tq // MXU_DIM

    def scores(t, slot, masked):
        qi, kj = qi_tab[t], kj_tab[t]
        krows = pl.ds(pl.multiple_of(kj * tk, tk), tk)
        diag = (kj == qi).astype(jnp.int32) if masked else None

        def piece(hd, c):
            qrows = pl.ds(pl.multiple_of(qi * tq + c * MXU_DIM, MXU_DIM), MXU_DIM)
            _score_strip(k_refs[hd][krows, :], q_refs[hd][qrows, :], lim_ref, diag, slots[slot], hd, c)

        return [functools.partial(piece, hd, c) for hd in range(2) for c in range(n_strips)]

    def update(t, slot):
        qi, kj = qi_tab[t], kj_tab[t]

        def piece(hd, c):
            _softmax_pv_strip(slots[slot], hd, c, vt_ref[kj, hd * VT_ROWS:(hd + 1) * VT_ROWS, :], kj == 0,
                              m_ref, acc_ref, qi)

        return [functools.partial(piece, hd, c) for hd in range(2) for c in range(n_strips)]

    _flat_sweep(nq * (nq + 1) // 2, trip_diag, scores, update)

    def finish(qb, carry):
        rows = pl.ds(pl.multiple_of(qb * tq, tq), tq)
        o_t = jnp.concatenate([_normalised_t(acc_ref, qb, 0), _normalised_t(acc_ref, qb, 1)], axis=0)
        o_ref[rows, :] = (o_t.T * gate_ref[rows, :].astype(F32)).astype(BF16)
        return carry

    lax.fori_loop(0, nq, finish, 0)


def _mla_attn_call(q, k, vt, gate, batch, seq):
    tq, tk = TQ_MLA, TK
    nq = seq // tq
    pairs = MLA_HEADS // 2
    step_tabs = _causal_steps(nq)
    blk = lambda b, hp, *_: (b, hp)
    head0 = pl.BlockSpec((seq, HEAD_PAD), lambda b, hp, *_: (b, 2 * hp))
    head1 = pl.BlockSpec((seq, HEAD_PAD), lambda b, hp, *_: (b, 2 * hp + 1))
    grid_spec = pltpu.PrefetchScalarGridSpec(
        num_scalar_prefetch=len(step_tabs),
        grid=(batch, pairs),
        in_specs=[head0, head1, head0, head1,
                  pl.BlockSpec((seq // tk, 2 * VT_ROWS, tk), lambda b, hp, *_: (b, hp, 0)),
                  pl.BlockSpec((seq, LANES), blk)],
        out_specs=pl.BlockSpec((seq, LANES), blk),
        scratch_shapes=_attn_scratch(2, nq, tk, tq))
    return pl.pallas_call(
        functools.partial(_mla_attn_kernel, tq=tq, tk=tk, nq=nq),
        grid_spec=grid_spec,
        out_shape=jax.ShapeDtypeStruct((batch * seq, MLA_W), BF16),
        compiler_params=pltpu.CompilerParams(dimension_semantics=("arbitrary",) * 2, vmem_limit_bytes=VMEM_LIMIT),
        name="mla_attn",
    )(*step_tabs, q, q, k, k, vt, gate)


def _diff_attn_kernel(qi_tab, kj_tab, trip_diag, q_ref, k_ref, vt_ref, gate_ref, lam_ref, sg_ref, o_ref, *scratch,
                      tq, tk, nq, lam_init):
    slots, lim_ref, m_ref, acc_ref = _split_scratch(scratch, tk)
    n_strips = tq // MXU_DIM
    map_of_lane = _lane_iota((1, LANES)) // DIFF_D

    def scores(t, slot, masked):
        qi, kj = qi_tab[t], kj_tab[t]
        krows = pl.ds(pl.multiple_of(kj * tk, tk), tk)
        diag = (kj == qi).astype(jnp.int32) if masked else None

        def piece(idx, c):
            q_strip = q_ref[pl.ds(pl.multiple_of(qi * tq + c * MXU_DIM, MXU_DIM), MXU_DIM), :]
            q_map = jnp.where(map_of_lane == idx, q_strip, jnp.zeros_like(q_strip))
            _score_strip(k_ref[krows, :], q_map, lim_ref, diag, slots[slot], idx, c)

        return [functools.partial(piece, idx, c) for idx in range(4) for c in range(n_strips)]

    def update(t, slot):
        qi, kj = qi_tab[t], kj_tab[t]

        def piece(idx, c):
            hd = idx // 2
            _softmax_pv_strip(slots[slot], idx, c, vt_ref[kj, hd * VT_ROWS:(hd + 1) * VT_ROWS, :], kj == 0,
                              m_ref, acc_ref, qi)

        return [functools.partial(piece, idx, c) for idx in range(4) for c in range(n_strips)]

    _flat_sweep(nq * (nq + 1) // 2, trip_diag, scores, update)

    lv = lam_ref[...]
    lam = (jnp.exp(jnp.sum(lv[0:1] * lv[1:2], axis=-1, keepdims=True))
           - jnp.exp(jnp.sum(lv[2:3] * lv[3:4], axis=-1, keepdims=True)) + lam_init)

    def finish(qb, carry):
        rows = pl.ds(pl.multiple_of(qb * tq, tq), tq)
        heads_t = []
        for hd in range(2):
            o = _normalised_t(acc_ref, qb, 2 * hd) - lam * _normalised_t(acc_ref, qb, 2 * hd + 1)
            heads_t.append(o * lax.rsqrt(jnp.mean(o * o, axis=0, keepdims=True) + RMS_EPS))
        o = jnp.concatenate(heads_t, axis=0).T * sg_ref[...] * (1.0 - lam_init)
        o_ref[rows, :] = (o * gate_ref[rows, :].astype(F32)).astype(BF16)
        return carry

    lax.fori_loop(0, nq, finish, 0)


def _diff_attn_call(q, k, vt, gate, lam_pad, subln_t, batch, seq, lam_init):
    tq, tk = TQ_DIFF, TK
    nq = seq // tq
    pairs = DIFF_HEADS // 2
    gate_off = MLA_W // LANES
    step_tabs = _causal_steps(nq)
    blk = lambda b, hp, *_: (b, hp)
    const = lambda b, hp, *_: (0, 0)
    grid_spec = pltpu.PrefetchScalarGridSpec(
        num_scalar_prefetch=len(step_tabs),
        grid=(batch, pairs),
        in_specs=[pl.BlockSpec((seq, LANES), blk),
                  pl.BlockSpec((seq, LANES), blk),
                  pl.BlockSpec((seq // tk, 2 * VT_ROWS, tk), lambda b, hp, *_: (b, hp, 0)),
                  pl.BlockSpec((seq, LANES), lambda b, hp, *_: (b, gate_off + hp)),
                  pl.BlockSpec(lam_pad.shape, const),
                  pl.BlockSpec(subln_t.shape, const)],
        out_specs=pl.BlockSpec((seq, LANES), blk),
        scratch_shapes=_attn_scratch(4, nq, tk, tq))
    return pl.pallas_call(
        functools.partial(_diff_attn_kernel, tq=tq, tk=tk, nq=nq, lam_init=lam_init),
        grid_spec=grid_spec,
        out_shape=jax.ShapeDtypeStruct((batch * seq, DIFF_W), BF16),
        compiler_params=pltpu.CompilerParams(dimension_semantics=("arbitrary",) * 2, vmem_limit_bytes=VMEM_LIMIT),
        name="diff_attn",
    )(*step_tabs, q, k, vt, gate, lam_pad, subln_t)


def _out_kernel(x_ref, ymla_ref, ydiff_ref, ymem_ref, w_ref, o_ref):
    y = jnp.concatenate([ymla_ref[...], ydiff_ref[...], ymem_ref[...]], axis=1)
    o_ref[...] = x_ref[...] + jnp.dot(y, w_ref[...], preferred_element_type=F32)


def _out_call(x2d, y_mla, y_diff, y_mem, w_out):
    n = x2d.shape[0]
    tm = TM_OUT
    row = lambda i: (i, 0)
    return pl.pallas_call(
        _out_kernel,
        grid=(n // tm,),
        in_specs=[pl.BlockSpec((tm, D_MODEL), row), pl.BlockSpec((tm, MLA_W), row),
                  pl.BlockSpec((tm, DIFF_W), row), pl.BlockSpec((tm, MEM_W), row),
                  pl.BlockSpec(w_out.shape, lambda i: (0, 0))],
        out_specs=pl.BlockSpec((tm, D_MODEL), row),
        out_shape=jax.ShapeDtypeStruct((n, D_MODEL), F32),
        compiler_params=pltpu.CompilerParams(dimension_semantics=("arbitrary",), vmem_limit_bytes=VMEM_LIMIT),
        name="out_proj",
    )(x2d, y_mla, y_diff, y_mem, w_out)


def _block_diag_ones(seg):
    idx = np.arange(MXU_DIM) // seg
    return jnp.asarray(idx[:, None] == idx[None, :], dtype=BF16)


def _pair_swapped(a):
    lead = a.shape[:-1]
    a4 = a.reshape(lead + (a.shape[-1] // ROPE_D, 2, ROPE_D // 2))
    return a4[..., ::-1, :].reshape(a.shape)


def _head_padded(a, heads, width, offset=0):
    rows = a.shape[0]
    a3 = a.reshape(rows, heads, width)
    return jnp.pad(a3, ((0, 0), (0, 0), (offset, LANES - width - offset))).reshape(rows, heads * LANES)


def _rope_lane_vec(g_rope):
    return jnp.pad(g_rope, (MLA_NOPE, LANES - MLA_QK))[None, :]


def _layer_params(l, norm_g, w_in, mla_q_norm_g, mla_kv_norm_g, w_uq, w_ukv, mla_qn_g, mla_kn_g,
                  diff_qn_g, diff_kn_g, mem_qn_g):
    w = w_in[l]
    o_cq, o_ckv, o_kr = 0, Q_LORA, Q_LORA + KV_LORA
    o_dq = o_kr + MLA_ROPE
    o_dk = o_dq + DIFF_QK_W
    o_dv = o_dk + DIFF_QK_W
    o_mq = o_dv + DIFF_W
    o_z = o_mq + MEM_W
    w_kr, w_dq, w_dk = w[:, o_kr:o_dq], w[:, o_dq:o_dk], w[:, o_dk:o_dv]
    w_in_p = jnp.concatenate([
        w[:, o_cq:o_ckv], w[:, o_ckv:o_kr],
        _head_padded(w_kr, 1, MLA_ROPE, MLA_NOPE), _head_padded(_pair_swapped(w_kr), 1, MLA_ROPE, MLA_NOPE),
        w_dq, _pair_swapped(w_dq), w_dk, _pair_swapped(w_dk),
        _head_padded(w[:, o_dv:o_mq], DIFF_HEADS, DIFF_V),
        w[:, o_mq:o_z], w[:, o_z:]], axis=1).astype(BF16)

    uq3 = w_uq[l].reshape(Q_LORA, MLA_HEADS, MLA_QK)
    w_uq_p = _head_padded(w_uq[l], MLA_HEADS, MLA_QK).astype(BF16)
    w_uq_sw = _head_padded(_pair_swapped(uq3[:, :, MLA_NOPE:]).reshape(Q_LORA, MLA_HEADS * MLA_ROPE),
                           MLA_HEADS, MLA_ROPE, MLA_NOPE).astype(BF16)
    kv3 = w_ukv[l].reshape(KV_LORA, MLA_HEADS, MLA_NOPE + MLA_V)
    w_uk_p = _head_padded(kv3[:, :, :MLA_NOPE].reshape(KV_LORA, MLA_HEADS * MLA_NOPE), MLA_HEADS, MLA_NOPE).astype(BF16)
    w_uv_t = jnp.pad(jnp.transpose(kv3[:, :, MLA_NOPE:], (1, 2, 0)),
                     ((0, 0), (0, VT_ROWS - MLA_V), (0, 0))).reshape(MLA_HEADS * VT_ROWS, KV_LORA).astype(BF16)

    q_scale = LOG2_E / math.sqrt(MLA_QK)
    d_scale = LOG2_E / math.sqrt(DIFF_D)
    qn, kn = mla_qn_g[l], mla_kn_g[l]
    nope_vec = lambda g: jnp.pad(g[:MLA_NOPE], (0, LANES - MLA_NOPE))[None, :]
    dqn = jnp.tile(diff_qn_g[l], 2 * DIFF_HEADS)[None, :]
    dkn = jnp.tile(diff_kn_g[l], 2 * DIFF_HEADS)[None, :]
    return {
        "gx": norm_g[l][None, :],
        "w_in": w_in_p,
        "gcq": mla_q_norm_g[l][None, :],
        "gckv": mla_kv_norm_g[l][None, :],
        "w_uq": w_uq_p, "w_uq_sw": w_uq_sw, "w_uk": w_uk_p, "w_uv_t": w_uv_t,
        "q_gc1": _rope_lane_vec(qn[MLA_NOPE:]) * q_scale, "q_gc0": nope_vec(qn) * q_scale,
        "q_gs": _rope_lane_vec(_pair_swapped(qn[MLA_NOPE:])) * q_scale,
        "k_gc1": _rope_lane_vec(kn[MLA_NOPE:]), "k_gc0": nope_vec(kn),
        "k_gs": _rope_lane_vec(_pair_swapped(kn[MLA_NOPE:])),
        "dq_gc": dqn * d_scale, "dq_gs": _pair_swapped(dqn) * d_scale,
        "dk_gc": dkn, "dk_gs": _pair_swapped(dkn),
        "mqn": jnp.tile(mem_qn_g[l], MEM_HEADS)[None, :] * (1.0 / math.sqrt(MEM_D)),
        "bd_diff": _block_diag_ones(DIFF_D), "bd_mem": _block_diag_ones(MEM_D),
    }


def kernel(x, mem, positions, norm_g, w_in, mla_q_norm_g, mla_kv_norm_g, w_uq, w_ukv, mla_qn_g, mla_kn_g,
           diff_qn_g, diff_kn_g, diff_lambda, diff_subln_g, mem_norm_g, w_mem_kv, mem_qn_g, mem_kn_g, w_out):
    batch, seq, d_model = x.shape
    depth = w_in.shape[0]
    assert d_model == D_MODEL and MLA_V == DIFF_V and TQ_MLA == TK and TQ_DIFF == TK
    assert TRIP_STEPS % SCORE_SLOTS == 0 and SCORE_SLOTS >= 2 and TK % MXU_DIM == 0
    assert seq % max(TM_PROJ, TQ_MLA, TQ_DIFF, TM_OUT) == 0 and (batch * seq) % 1024 == 0
    cos_t, sin_t = _rope_tables(positions)
    kmem_all, vmem_all = _mem_kv(mem, mem_norm_g, w_mem_kv, mem_kn_g, _block_diag_ones(MEM_D))
    x2d = x.reshape(batch * seq, D_MODEL)
    for l in range(depth):
        lam_init = 0.8 - 0.6 * math.exp(-0.3 * l)
        p = _layer_params(l, norm_g, w_in, mla_q_norm_g, mla_kv_norm_g, w_uq, w_ukv, mla_qn_g, mla_kn_g,
                          diff_qn_g, diff_kn_g, mem_qn_g)
        q, k, v, dq, dk, dv, y_mem, gate = _proj_call(x2d, cos_t, sin_t, kmem_all[l], vmem_all[l], p, seq)
        y_mla = _mla_attn_call(q, k, v, gate, batch, seq)
        lam_pad = jnp.pad(diff_lambda[l], ((0, 4), (0, LANES - DIFF_D)))
        subln_t = jnp.tile(diff_subln_g[l], 2)[None, :]
        y_diff = _diff_attn_call(dq, dk, dv, gate, lam_pad, subln_t, batch, seq, lam_init)
        x2d = _out_call(x2d, y_mla, y_diff, y_mem, w_out[l].astype(BF16))
    return x2d.reshape(batch, seq, D_MODEL)
```

```python
import functools
import math

import numpy as np
import jax
import jax.numpy as jnp
from jax import lax
from jax.experimental import pallas as pl
from jax.experimental.pallas import tpu as pltpu

F32 = jnp.float32
BF16 = jnp.bfloat16

D_MODEL = 1024
ROPE_THETA = 10000.0
RMS_EPS = 1e-6
NEG_INF = -1e30
LOG2_E = math.log2(math.e)

MLA_HEADS = 8
Q_LORA = 256
KV_LORA = 128
MLA_NOPE = 64
MLA_ROPE = 32
MLA_QK = MLA_NOPE + MLA_ROPE
MLA_V = 64
MLA_W = MLA_HEADS * MLA_V

DIFF_HEADS = 4
DIFF_D = 32
DIFF_V = 2 * DIFF_D
DIFF_W = DIFF_HEADS * DIFF_V
DIFF_QK_W = 2 * DIFF_HEADS * DIFF_D

MEM_HEADS = 4
MEM_D = 64
MEM_W = MEM_HEADS * MEM_D

D_MIX = MLA_W + DIFF_W + MEM_W

LANES = 128
MXU_DIM = 256
ROPE_D = 32
HEAD_PAD = LANES

_C_CQ = 0
_C_CKV = _C_CQ + Q_LORA
_C_KR = _C_CKV + KV_LORA
_C_KR_SW = _C_KR + LANES
_C_DQ = _C_KR_SW + LANES
_C_DQ_SW = _C_DQ + DIFF_QK_W
_C_DK = _C_DQ_SW + DIFF_QK_W
_C_DK_SW = _C_DK + DIFF_QK_W
_C_DV = _C_DK_SW + DIFF_QK_W
_C_MQ = _C_DV + DIFF_HEADS * LANES
_C_Z = _C_MQ + MEM_W
_C_END = _C_Z + D_MIX

VMEM_LIMIT = 56 * 1024 * 1024

V_DIM = MLA_V
VT_ROWS = 80

TK = 512
TM_PROJ = TK
TM_OUT = 512
TQ_MLA = 512
TQ_DIFF = 512
SCORE_SLOTS = 2
TRIP_STEPS = 4
SCORE_CEIL = 3.0e38

_NT = (((1,), (1,)), ((), ()))


def _rms(x):
    return x * lax.rsqrt(jnp.mean(x * x, axis=-1, keepdims=True) + RMS_EPS)


def _lane_iota(shape):
    return lax.broadcasted_iota(jnp.int32, shape, len(shape) - 1)


def _seg_sum_sq(x, bd_ref):
    sq = x * x
    hi = sq.astype(BF16)
    lo = (sq - hi.astype(F32)).astype(BF16)
    bd = bd_ref[...]
    return jnp.dot(hi, bd, preferred_element_type=F32) + jnp.dot(lo, bd, preferred_element_type=F32)


def _rope_table_kernel(pos_ref, inv_ref, cos_ref, sin_ref):
    ang = pos_ref[...].astype(F32) * inv_ref[...]
    lane = _lane_iota(ang.shape)
    sign = jnp.where((lane & (ROPE_D // 2)) == 0, -1.0, 1.0)
    cos_ref[...] = jnp.cos(ang)
    sin_ref[...] = jnp.sin(ang) * sign


def _rope_tables(positions):
    n = positions.size
    tm = 1024
    inv = ROPE_THETA ** (-jnp.arange(0, ROPE_D, 2, dtype=F32) / ROPE_D)
    inv_t = jnp.tile(inv, LANES // (ROPE_D // 2))[None, :]
    pos = positions.reshape(n, 1)
    return pl.pallas_call(
        _rope_table_kernel,
        grid=(n // tm,),
        in_specs=[pl.BlockSpec((tm, 1), lambda i: (i, 0)),
                  pl.BlockSpec((1, LANES), lambda i: (0, 0))],
        out_specs=[pl.BlockSpec((tm, LANES), lambda i: (i, 0)),
                   pl.BlockSpec((tm, LANES), lambda i: (i, 0))],
        out_shape=[jax.ShapeDtypeStruct((n, LANES), F32)] * 2,
        name="rope_tables",
    )(pos, inv_t)


def _mem_kv_kernel(mem_ref, g_ref, w_ref, kn_ref, bd_ref, k_ref, v_ref):
    m = mem_ref[0]
    mn = (_rms(m) * g_ref[0]).astype(BF16)
    kv = jnp.dot(mn, w_ref[0], preferred_element_type=F32)
    k = kv[:, :MEM_W]
    k = k * lax.rsqrt(_seg_sum_sq(k, bd_ref) * (1.0 / MEM_D) + RMS_EPS) * kn_ref[0]
    k_ref[0, 0] = k.astype(BF16)
    v_ref[0, 0] = kv[:, MEM_W:].astype(BF16)


def _mem_kv(mem, mem_norm_g, w_mem_kv, mem_kn_g, bd_mem):
    depth = w_mem_kv.shape[0]
    b, m, d = mem.shape
    kn_t = jnp.tile(mem_kn_g, (1, MEM_HEADS))[:, None, :]
    return pl.pallas_call(
        _mem_kv_kernel,
        grid=(depth, b),
        in_specs=[pl.BlockSpec((1, m, d), lambda l, i: (i, 0, 0)),
                  pl.BlockSpec((1, 1, d), lambda l, i: (l, 0, 0)),
                  pl.BlockSpec((1, d, 2 * MEM_W), lambda l, i: (l, 0, 0)),
                  pl.BlockSpec((1, 1, MEM_W), lambda l, i: (l, 0, 0)),
                  pl.BlockSpec(bd_mem.shape, lambda l, i: (0, 0))],
        out_specs=[pl.BlockSpec((1, 1, m, MEM_W), lambda l, i: (l, i, 0, 0)),
                   pl.BlockSpec((1, 1, m, MEM_W), lambda l, i: (l, i, 0, 0))],
        out_shape=[jax.ShapeDtypeStruct((depth, b, m, MEM_W), BF16)] * 2,
        name="mem_kv",
    )(mem, mem_norm_g[:, None, :], w_mem_kv.astype(BF16), kn_t, bd_mem)


_PROJ_PARAMS = ("gx", "w_in", "gcq", "gckv", "w_uq", "w_uq_sw", "w_uk", "w_uv_t",
                "q_gc1", "q_gc0", "q_gs", "k_gc1", "k_gc0", "k_gs",
                "dq_gc", "dq_gs", "dk_gc", "dk_gs", "mqn", "bd_diff", "bd_mem")


def _proj_kernel(x_ref, cos_ref, sin_ref, gx_ref, win_ref, gcq_ref, gckv_ref, wuq_ref, wuqsw_ref, wuk_ref, wuvt_ref,
                 qgc1_ref, qgc0_ref, qgs_ref, kgc1_ref, kgc0_ref, kgs_ref,
                 dqgc_ref, dqgs_ref, dkgc_ref, dkgs_ref, mqn_ref, bdd_ref, bdm_ref, kmem_ref, vmem_ref,
                 q_out, k_out, v_out, dq_out, dk_out, dv_out, ymem_out, gate_out):
    tm = x_ref.shape[0]
    h = (_rms(x_ref[...]) * gx_ref[...]).astype(BF16)
    proj = jnp.dot(h, win_ref[...], preferred_element_type=F32)
    cos = cos_ref[...]
    sin = sin_ref[...]

    cqn = (_rms(proj[:, _C_CQ:_C_CQ + Q_LORA]) * gcq_ref[...]).astype(BF16)
    q = jnp.dot(cqn, wuq_ref[...], preferred_element_type=F32)
    q_sw = jnp.dot(cqn, wuqsw_ref[...], preferred_element_type=F32)
    q_gc = cos * qgc1_ref[...] + qgc0_ref[...]
    q_gs = sin * qgs_ref[...]
    for hd in range(MLA_HEADS):
        sl = slice(hd * HEAD_PAD, (hd + 1) * HEAD_PAD)
        qh = q[:, sl]
        r = lax.rsqrt(jnp.sum(qh * qh, axis=-1, keepdims=True) * (1.0 / MLA_QK) + RMS_EPS)
        q_out[:, sl] = ((qh * q_gc + q_sw[:, sl] * q_gs) * r).astype(BF16)

    ckv = _rms(proj[:, _C_CKV:_C_CKV + KV_LORA]) * gckv_ref[...]
    ckvn = ckv.astype(BF16)
    k_nope = jnp.dot(ckvn, wuk_ref[...], preferred_element_type=F32)
    kr = proj[:, _C_KR:_C_KR + LANES]
    k_gc = cos * kgc1_ref[...] + kgc0_ref[...]
    kr_rot = proj[:, _C_KR_SW:_C_KR_SW + LANES] * (sin * kgs_ref[...])
    for hd in range(MLA_HEADS):
        sl = slice(hd * HEAD_PAD, (hd + 1) * HEAD_PAD)
        kh = k_nope[:, sl] + kr
        r = lax.rsqrt(jnp.sum(kh * kh, axis=-1, keepdims=True) * (1.0 / MLA_QK) + RMS_EPS)
        k_out[:, sl] = ((kh * k_gc + kr_rot) * r).astype(BF16)
    vt = jnp.dot(wuvt_ref[...], ckv.T.astype(BF16), preferred_element_type=F32)
    ones_row = (lax.broadcasted_iota(jnp.int32, (VT_ROWS, tm), 0) == V_DIM).astype(F32)
    for hd in range(MLA_HEADS):
        rows = slice(hd * VT_ROWS, (hd + 1) * VT_ROWS)
        v_out[0, rows, :] = (vt[rows] + ones_row).astype(BF16)

    for raw_c, sw_c, gc_ref, gs_ref, out in ((_C_DQ, _C_DQ_SW, dqgc_ref, dqgs_ref, dq_out),
                                             (_C_DK, _C_DK_SW, dkgc_ref, dkgs_ref, dk_out)):
        raw = proj[:, raw_c:raw_c + DIFF_QK_W]
        r = lax.rsqrt(_seg_sum_sq(raw, bdd_ref) * (1.0 / DIFF_D) + RMS_EPS)
        for c in range(DIFF_QK_W // LANES):
            sl = slice(c * LANES, (c + 1) * LANES)
            rot = raw[:, sl] * (cos * gc_ref[:, sl]) + proj[:, sw_c + c * LANES:sw_c + (c + 1) * LANES] * (sin * gs_ref[:, sl])
            out[:, sl] = (rot * r[:, sl]).astype(BF16)
    ones_lane = (_lane_iota((1, LANES)) == V_DIM).astype(F32)
    for hd in range(DIFF_HEADS):
        dv_t = (proj[:, _C_DV + hd * LANES:_C_DV + (hd + 1) * LANES] + ones_lane).T
        dv_out[0, hd * VT_ROWS:(hd + 1) * VT_ROWS, :] = dv_t[:VT_ROWS].astype(BF16)

    z = proj[:, _C_Z:_C_END]
    gate = z * (1.0 / (1.0 + jnp.exp(-z)))
    gate_out[...] = gate[:, :MLA_W + DIFF_W].astype(BF16)

    mq = proj[:, _C_MQ:_C_MQ + MEM_W]
    mq = mq * lax.rsqrt(_seg_sum_sq(mq, bdm_ref) * (1.0 / MEM_D) + RMS_EPS) * mqn_ref[...]
    kmem = kmem_ref[0]
    vmem = vmem_ref[0]
    lane_w = _lane_iota((1, MEM_W))
    o = jnp.zeros(mq.shape, F32)
    for hd in range(MEM_HEADS):
        head_mask = (lane_w // MEM_D) == hd
        qh = jnp.where(head_mask, mq, 0.0).astype(BF16)
        s = lax.dot_general(qh, kmem, _NT, preferred_element_type=F32)
        p = jnp.exp(s - jnp.max(s, axis=-1, keepdims=True))
        inv_l = 1.0 / jnp.sum(p, axis=-1, keepdims=True)
        vh = jnp.where(head_mask, vmem, jnp.zeros_like(vmem))
        o = o + jnp.dot(p.astype(BF16), vh, preferred_element_type=F32) * inv_l
    ymem_out[...] = (o * gate[:, MLA_W + DIFF_W:]).astype(BF16)


def _proj_call(x2d, cos_t, sin_t, kmem, vmem, p, layer, seq):
    n = x2d.shape[0]
    tm = TM_PROJ
    per_batch = seq // tm
    row = lambda i: (i, 0)
    params = [p[name] for name in _PROJ_PARAMS]
    mem_spec = pl.BlockSpec((None, 1) + kmem.shape[2:], lambda i: (layer, i // per_batch, 0, 0))
    in_specs = ([pl.BlockSpec((tm, D_MODEL), row), pl.BlockSpec((tm, LANES), row), pl.BlockSpec((tm, LANES), row)]
                + [_layer_spec(a, layer) for a in params] + [mem_spec, mem_spec])

    def rows_out(width):
        return pl.BlockSpec((tm, width), row), jax.ShapeDtypeStruct((n, width), BF16)

    def values_t_out(heads):
        return (pl.BlockSpec((1, heads * VT_ROWS, tm), lambda i: (i, 0, 0)),
                jax.ShapeDtypeStruct((n // tm, heads * VT_ROWS, tm), BF16))

    outs = [rows_out(MLA_HEADS * HEAD_PAD), rows_out(MLA_HEADS * HEAD_PAD), values_t_out(MLA_HEADS),
            rows_out(DIFF_QK_W), rows_out(DIFF_QK_W), values_t_out(DIFF_HEADS),
            rows_out(MEM_W), rows_out(MLA_W + DIFF_W)]
    return pl.pallas_call(
        _proj_kernel,
        grid=(n // tm,),
        in_specs=in_specs,
        out_specs=[o[0] for o in outs],
        out_shape=[o[1] for o in outs],
        compiler_params=pltpu.CompilerParams(dimension_semantics=("arbitrary",), vmem_limit_bytes=VMEM_LIMIT),
        name="proj_prep",
    )(x2d, cos_t, sin_t, *params, kmem, vmem)


def _causal_keep_t(tk, tq, key_offset):
    r = lax.broadcasted_iota(jnp.int32, (tk, tq), 0)
    c = lax.broadcasted_iota(jnp.int32, (tk, tq), 1)
    return (r + key_offset) <= c


def _init_limits(lim_ref, tk):
    for c in range(lim_ref.shape[1]):
        lim_ref[0, c] = jnp.full((tk, MXU_DIM), SCORE_CEIL, F32)
        lim_ref[1, c] = jnp.where(_causal_keep_t(tk, MXU_DIM, -c * MXU_DIM), SCORE_CEIL, NEG_INF)


def _score_strip(k_blk, q_strip, lim_ref, diag, slot, idx, c):
    s_ref, mb_ref = slot
    s = lax.dot_general(k_blk, q_strip, _NT, preferred_element_type=F32)
    if diag is not None:
        s = jnp.minimum(s, lim_ref[diag, c])
    s_ref[idx, c] = s
    mb_ref[idx, c] = jnp.max(s, axis=0, keepdims=True)


def _softmax_pv_strip(slot, idx, c, vt_blk, first, m_ref, acc_ref, qi):
    s_ref, mb_ref = slot
    s = s_ref[idx, c]
    m_old = jnp.where(first, NEG_INF, m_ref[qi, idx, c])
    m_new = jnp.maximum(m_old, mb_ref[idx, c])
    alpha = jnp.exp2(m_old - m_new)
    p = jnp.exp2(s - m_new).astype(BF16)
    acc_ref[qi, idx, c] = acc_ref[qi, idx, c] * alpha + jnp.dot(vt_blk, p, preferred_element_type=F32)
    m_ref[qi, idx, c] = m_new


def _causal_steps(nq):
    below = [(i, j) for i in range(nq) for j in range(i)]
    qi = np.array([i for i in range(nq)] + [i for i, _ in below], np.int32)
    kj = np.array([i for i in range(nq)] + [j for _, j in below], np.int32)
    first = (np.arange(len(qi)) < nq).astype(np.int32)
    trips = (len(qi) - 1) // TRIP_STEPS
    on_diag = (qi == kj)[1:1 + trips * TRIP_STEPS].reshape(trips, TRIP_STEPS)
    return qi, kj, first, np.append(on_diag.any(axis=1), False).astype(np.int32)


def _flat_sweep(n_steps, trip_diag, scores, update):
    def pair(t0, u, with_scores, masked):
        ahead = scores(t0 + u + 1, (u + 1) % SCORE_SLOTS, masked) if with_scores else []
        now = update(t0 + u, u % SCORE_SLOTS)
        for i in range(max(len(ahead), len(now))):
            if i < len(ahead):
                ahead[i]()
            if i < len(now):
                now[i]()

    for piece in scores(0, 0, True):
        piece()
    trips = (n_steps - 1) // TRIP_STEPS

    def trip(i, masked):
        for u in range(TRIP_STEPS):
            pair(i * TRIP_STEPS, u, True, masked)

    def body(i, carry):
        lax.cond(trip_diag[i] != 0, functools.partial(trip, i, True), functools.partial(trip, i, False))
        return carry

    lax.fori_loop(0, trips, body, 0)
    done = trips * TRIP_STEPS
    for u in range(n_steps - done):
        pair(done, u, done + u + 1 < n_steps, True)


def _normalised_t(acc_ref, qb, idx):
    strips = []
    for c in range(acc_ref.shape[2]):
        acc = acc_ref[qb, idx, c]
        strips.append(acc[:V_DIM] * (1.0 / acc[V_DIM:V_DIM + 1]))
    return jnp.concatenate(strips, axis=1)


def _attn_scratch(count, nq, tk, tq):
    ns = tq // MXU_DIM
    shapes = []
    for _ in range(SCORE_SLOTS):
        shapes += [pltpu.VMEM((count, ns, tk, MXU_DIM), F32), pltpu.VMEM((count, ns, 1, MXU_DIM), F32)]
    return shapes + [pltpu.VMEM((2, ns, tk, MXU_DIM), F32), pltpu.VMEM((nq, count, ns, 1, MXU_DIM), F32),
                     pltpu.VMEM((nq, count, ns, VT_ROWS, MXU_DIM), F32)]


def _split_scratch(scratch, tk):
    slots = [(scratch[2 * i], scratch[2 * i + 1]) for i in range(SCORE_SLOTS)]
    lim_ref, m_ref, acc_ref = scratch[2 * SCORE_SLOTS:]
    _init_limits(lim_ref, tk)
    m_ref[...] = jnp.full(m_ref.shape, NEG_INF, F32)
    acc_ref[...] = jnp.zeros(acc_ref.shape, F32)
    return slots, lim_ref, m_ref, acc_ref


def _mla_attn_kernel(qi_tab, kj_tab, first_tab, trip_diag, q0_ref, q1_ref, k0_ref, k1_ref, vt_ref, gate_ref, o_ref,
                     *scratch, tq, tk, nq):
    slots, lim_ref, m_ref, acc_ref = _split_scratch(scratch, tk)
    q_refs, k_refs = (q0_ref, q1_ref), (k0_ref, k1_ref)
    n_strips = tq // MXU_DIM

    def scores(t, slot, masked):
        qi, kj = qi_tab[t], kj_tab[t]
        krows = pl.ds(pl.multiple_of(kj * tk, tk), tk)
        diag = (kj == qi).astype(jnp.int32) if masked else None

        def piece(hd, c):
            qrows = pl.ds(pl.multiple_of(qi * tq + c * MXU_DIM, MXU_DIM), MXU_DIM)
            _score_strip(k_refs[hd][krows, :], q_refs[hd][qrows, :], lim_ref, diag, slots[slot], hd, c)

        return [functools.partial(piece, hd, c) for hd in range(2) for c in range(n_strips)]

    def update(t, slot):
        qi, kj = qi_tab[t], kj_tab[t]

        def piece(hd, c):
            _softmax_pv_strip(slots[slot], hd, c, vt_ref[kj, hd * VT_ROWS:(hd + 1) * VT_ROWS, :],
                              first_tab[t] != 0, m_ref, acc_ref, qi)

        return [functools.partial(piece, hd, c) for hd in range(2) for c in range(n_strips)]

    _flat_sweep(nq * (nq + 1) // 2, trip_diag, scores, update)

    def finish(qb, carry):
        rows = pl.ds(pl.multiple_of(qb * tq, tq), tq)
        o_t = jnp.concatenate([_normalised_t(acc_ref, qb, 0), _normalised_t(acc_ref, qb, 1)], axis=0)
        o_ref[rows, :] = (o_t.T * gate_ref[rows, :].astype(F32)).astype(BF16)
        return carry

    lax.fori_loop(0, nq, finish, 0)


def _mla_attn_call(q, k, vt, gate, batch, seq):
    tq, tk = TQ_MLA, TK
    nq = seq // tq
    pairs = MLA_HEADS // 2
    step_tabs = _causal_steps(nq)
    blk = lambda b, hp, *_: (b, hp)
    head0 = pl.BlockSpec((seq, HEAD_PAD), lambda b, hp, *_: (b, 2 * hp))
    head1 = pl.BlockSpec((seq, HEAD_PAD), lambda b, hp, *_: (b, 2 * hp + 1))
    grid_spec = pltpu.PrefetchScalarGridSpec(
        num_scalar_prefetch=len(step_tabs),
        grid=(batch, pairs),
        in_specs=[head0, head1, head0, head1,
                  pl.BlockSpec((seq // tk, 2 * VT_ROWS, tk), lambda b, hp, *_: (b, hp, 0)),
                  pl.BlockSpec((seq, LANES), blk)],
        out_specs=pl.BlockSpec((seq, LANES), blk),
        scratch_shapes=_attn_scratch(2, nq, tk, tq))
    return pl.pallas_call(
        functools.partial(_mla_attn_kernel, tq=tq, tk=tk, nq=nq),
        grid_spec=grid_spec,
        out_shape=jax.ShapeDtypeStruct((batch * seq, MLA_W), BF16),
        compiler_params=pltpu.CompilerParams(dimension_semantics=("arbitrary",) * 2, vmem_limit_bytes=VMEM_LIMIT),
        name="mla_attn",
    )(*step_tabs, q, q, k, k, vt, gate)


def _diff_attn_kernel(qi_tab, kj_tab, first_tab, trip_diag, q_ref, k_ref, vt_ref, gate_ref, lam_ref, sg_ref, o_ref,
                      *scratch, tq, tk, nq, lam_init):
    slots, lim_ref, m_ref, acc_ref = _split_scratch(scratch, tk)
    n_strips = tq // MXU_DIM
    map_of_lane = _lane_iota((1, LANES)) // DIFF_D

    def scores(t, slot, masked):
        qi, kj = qi_tab[t], kj_tab[t]
        krows = pl.ds(pl.multiple_of(kj * tk, tk), tk)
        diag = (kj == qi).astype(jnp.int32) if masked else None

        def piece(idx, c):
            q_strip = q_ref[pl.ds(pl.multiple_of(qi * tq + c * MXU_DIM, MXU_DIM), MXU_DIM), :]
            q_map = jnp.where(map_of_lane == idx, q_strip, jnp.zeros_like(q_strip))
            _score_strip(k_ref[krows, :], q_map, lim_ref, diag, slots[slot], idx, c)

        return [functools.partial(piece, idx, c) for idx in range(4) for c in range(n_strips)]

    def update(t, slot):
        qi, kj = qi_tab[t], kj_tab[t]

        def piece(idx, c):
            hd = idx // 2
            _softmax_pv_strip(slots[slot], idx, c, vt_ref[kj, hd * VT_ROWS:(hd + 1) * VT_ROWS, :],
                              first_tab[t] != 0, m_ref, acc_ref, qi)

        return [functools.partial(piece, idx, c) for idx in range(4) for c in range(n_strips)]

    _flat_sweep(nq * (nq + 1) // 2, trip_diag, scores, update)

    lv = lam_ref[...]
    lam = (jnp.exp(jnp.sum(lv[0:1] * lv[1:2], axis=-1, keepdims=True))
           - jnp.exp(jnp.sum(lv[2:3] * lv[3:4], axis=-1, keepdims=True)) + lam_init)

    def finish(qb, carry):
        rows = pl.ds(pl.multiple_of(qb * tq, tq), tq)
        heads_t = []
        for hd in range(2):
            o = _normalised_t(acc_ref, qb, 2 * hd) - lam * _normalised_t(acc_ref, qb, 2 * hd + 1)
            heads_t.append(o * lax.rsqrt(jnp.mean(o * o, axis=0, keepdims=True) + RMS_EPS))
        o = jnp.concatenate(heads_t, axis=0).T * sg_ref[...] * (1.0 - lam_init)
        o_ref[rows, :] = (o * gate_ref[rows, :].astype(F32)).astype(BF16)
        return carry

    lax.fori_loop(0, nq, finish, 0)


def _diff_attn_call(q, k, vt, gate, lam_pad, subln_t, layer, batch, seq, lam_init):
    tq, tk = TQ_DIFF, TK
    nq = seq // tq
    pairs = DIFF_HEADS // 2
    gate_off = MLA_W // LANES
    step_tabs = _causal_steps(nq)
    blk = lambda b, hp, *_: (b, hp)
    grid_spec = pltpu.PrefetchScalarGridSpec(
        num_scalar_prefetch=len(step_tabs),
        grid=(batch, pairs),
        in_specs=[pl.BlockSpec((seq, LANES), blk),
                  pl.BlockSpec((seq, LANES), blk),
                  pl.BlockSpec((seq // tk, 2 * VT_ROWS, tk), lambda b, hp, *_: (b, hp, 0)),
                  pl.BlockSpec((seq, LANES), lambda b, hp, *_: (b, gate_off + hp)),
                  _layer_spec(lam_pad, layer),
                  _layer_spec(subln_t, layer)],
        out_specs=pl.BlockSpec((seq, LANES), blk),
        scratch_shapes=_attn_scratch(4, nq, tk, tq))
    return pl.pallas_call(
        functools.partial(_diff_attn_kernel, tq=tq, tk=tk, nq=nq, lam_init=lam_init),
        grid_spec=grid_spec,
        out_shape=jax.ShapeDtypeStruct((batch * seq, DIFF_W), BF16),
        compiler_params=pltpu.CompilerParams(dimension_semantics=("arbitrary",) * 2, vmem_limit_bytes=VMEM_LIMIT),
        name="diff_attn",
    )(*step_tabs, q, k, vt, gate, lam_pad, subln_t)


def _out_kernel(x_ref, ymla_ref, ydiff_ref, ymem_ref, w_ref, o_ref):
    y = jnp.concatenate([ymla_ref[...], ydiff_ref[...], ymem_ref[...]], axis=1)
    o_ref[...] = x_ref[...] + jnp.dot(y, w_ref[...], preferred_element_type=F32)


def _out_call(x2d, y_mla, y_diff, y_mem, w_out, layer):
    n = x2d.shape[0]
    tm = TM_OUT
    row = lambda i: (i, 0)
    return pl.pallas_call(
        _out_kernel,
        grid=(n // tm,),
        in_specs=[pl.BlockSpec((tm, D_MODEL), row), pl.BlockSpec((tm, MLA_W), row),
                  pl.BlockSpec((tm, DIFF_W), row), pl.BlockSpec((tm, MEM_W), row),
                  _layer_spec(w_out, layer)],
        out_specs=pl.BlockSpec((tm, D_MODEL), row),
        out_shape=jax.ShapeDtypeStruct((n, D_MODEL), F32),
        compiler_params=pltpu.CompilerParams(dimension_semantics=("arbitrary",), vmem_limit_bytes=VMEM_LIMIT),
        name="out_proj",
    )(x2d, y_mla, y_diff, y_mem, w_out)


def _block_diag_ones(seg):
    idx = np.arange(MXU_DIM) // seg
    return jnp.asarray(idx[:, None] == idx[None, :], dtype=BF16)


def _pair_swapped(a):
    lead = a.shape[:-1]
    a4 = a.reshape(lead + (a.shape[-1] // ROPE_D, 2, ROPE_D // 2))
    return a4[..., ::-1, :].reshape(a.shape)


def _head_padded(a, heads, width, offset=0):
    lead = a.shape[:-1]
    a3 = a.reshape(lead + (heads, width))
    pad = [(0, 0)] * (a3.ndim - 1) + [(offset, LANES - width - offset)]
    return jnp.pad(a3, pad).reshape(lead + (heads * LANES,))


def _layer_spec(a, layer):
    zeros = (0,) * (a.ndim - 1)
    return pl.BlockSpec((None,) + a.shape[1:], lambda *_: (layer,) + zeros)


def _stacked_params(norm_g, w_in, mla_q_norm_g, mla_kv_norm_g, w_uq, w_ukv, mla_qn_g, mla_kn_g,
                    diff_qn_g, diff_kn_g, mem_qn_g):
    depth = w_in.shape[0]
    o_cq, o_ckv, o_kr = 0, Q_LORA, Q_LORA + KV_LORA
    o_dq = o_kr + MLA_ROPE
    o_dk = o_dq + DIFF_QK_W
    o_dv = o_dk + DIFF_QK_W
    o_mq = o_dv + DIFF_W
    o_z = o_mq + MEM_W
    w = w_in
    w_kr, w_dq, w_dk = w[..., o_kr:o_dq], w[..., o_dq:o_dk], w[..., o_dk:o_dv]
    w_in_p = jnp.concatenate([
        w[..., o_cq:o_ckv], w[..., o_ckv:o_kr],
        _head_padded(w_kr, 1, MLA_ROPE, MLA_NOPE), _head_padded(_pair_swapped(w_kr), 1, MLA_ROPE, MLA_NOPE),
        w_dq, _pair_swapped(w_dq), w_dk, _pair_swapped(w_dk),
        _head_padded(w[..., o_dv:o_mq], DIFF_HEADS, DIFF_V),
        w[..., o_mq:o_z], w[..., o_z:]], axis=-1).astype(BF16)

    uq4 = w_uq.reshape(depth, Q_LORA, MLA_HEADS, MLA_QK)
    w_uq_p = _head_padded(w_uq, MLA_HEADS, MLA_QK).astype(BF16)
    w_uq_sw = _head_padded(_pair_swapped(uq4[..., MLA_NOPE:]).reshape(depth, Q_LORA, MLA_HEADS * MLA_ROPE),
                           MLA_HEADS, MLA_ROPE, MLA_NOPE).astype(BF16)
    kv4 = w_ukv.reshape(depth, KV_LORA, MLA_HEADS, MLA_NOPE + MLA_V)
    w_uk_p = _head_padded(kv4[..., :MLA_NOPE].reshape(depth, KV_LORA, MLA_HEADS * MLA_NOPE),
                          MLA_HEADS, MLA_NOPE).astype(BF16)
    w_uv_t = jnp.pad(jnp.transpose(kv4[..., MLA_NOPE:], (0, 2, 3, 1)),
                     ((0, 0), (0, 0), (0, VT_ROWS - MLA_V), (0, 0))
                     ).reshape(depth, MLA_HEADS * VT_ROWS, KV_LORA).astype(BF16)

    q_scale = LOG2_E / math.sqrt(MLA_QK)
    d_scale = LOG2_E / math.sqrt(DIFF_D)
    row = lambda v: v[:, None, :]
    rope_vec = lambda g: row(jnp.pad(g[:, MLA_NOPE:], ((0, 0), (MLA_NOPE, LANES - MLA_QK))))
    rope_vec_sw = lambda g: row(jnp.pad(_pair_swapped(g[:, MLA_NOPE:]), ((0, 0), (MLA_NOPE, LANES - MLA_QK))))
    nope_vec = lambda g: row(jnp.pad(g[:, :MLA_NOPE], ((0, 0), (0, LANES - MLA_NOPE))))
    dqn = row(jnp.tile(diff_qn_g, (1, 2 * DIFF_HEADS)))
    dkn = row(jnp.tile(diff_kn_g, (1, 2 * DIFF_HEADS)))
    return {
        "gx": row(norm_g),
        "w_in": w_in_p,
        "gcq": row(mla_q_norm_g),
        "gckv": row(mla_kv_norm_g),
        "w_uq": w_uq_p, "w_uq_sw": w_uq_sw, "w_uk": w_uk_p, "w_uv_t": w_uv_t,
        "q_gc1": rope_vec(mla_qn_g) * q_scale, "q_gc0": nope_vec(mla_qn_g) * q_scale,
        "q_gs": rope_vec_sw(mla_qn_g) * q_scale,
        "k_gc1": rope_vec(mla_kn_g), "k_gc0": nope_vec(mla_kn_g), "k_gs": rope_vec_sw(mla_kn_g),
        "dq_gc": dqn * d_scale, "dq_gs": _pair_swapped(dqn) * d_scale,
        "dk_gc": dkn, "dk_gs": _pair_swapped(dkn),
        "mqn": row(jnp.tile(mem_qn_g, (1, MEM_HEADS))) * (1.0 / math.sqrt(MEM_D)),
        "bd_diff": jnp.broadcast_to(_block_diag_ones(DIFF_D), (depth, MXU_DIM, MXU_DIM)),
        "bd_mem": jnp.broadcast_to(_block_diag_ones(MEM_D), (depth, MXU_DIM, MXU_DIM)),
    }


def kernel(x, mem, positions, norm_g, w_in, mla_q_norm_g, mla_kv_norm_g, w_uq, w_ukv, mla_qn_g, mla_kn_g,
           diff_qn_g, diff_kn_g, diff_lambda, diff_subln_g, mem_norm_g, w_mem_kv, mem_qn_g, mem_kn_g, w_out):
    batch, seq, d_model = x.shape
    depth = w_in.shape[0]
    assert d_model == D_MODEL and MLA_V == DIFF_V and TQ_MLA == TK and TQ_DIFF == TK
    assert TRIP_STEPS % SCORE_SLOTS == 0 and SCORE_SLOTS >= 2 and TK % MXU_DIM == 0
    assert seq % max(TM_PROJ, TQ_MLA, TQ_DIFF, TM_OUT) == 0 and (batch * seq) % 1024 == 0
    cos_t, sin_t = _rope_tables(positions)
    kmem_all, vmem_all = _mem_kv(mem, mem_norm_g, w_mem_kv, mem_kn_g, _block_diag_ones(MEM_D))
    p = _stacked_params(norm_g, w_in, mla_q_norm_g, mla_kv_norm_g, w_uq, w_ukv, mla_qn_g, mla_kn_g,
                        diff_qn_g, diff_kn_g, mem_qn_g)
    lam_pad = jnp.pad(diff_lambda, ((0, 0), (0, 4), (0, LANES - DIFF_D)))
    subln_t = jnp.tile(diff_subln_g, (1, 2))[:, None, :]
    w_out_b = w_out.astype(BF16)
    x2d = x.reshape(batch * seq, D_MODEL)
    for l in range(depth):
        lam_init = 0.8 - 0.6 * math.exp(-0.3 * l)
        q, k, v, dq, dk, dv, y_mem, gate = _proj_call(x2d, cos_t, sin_t, kmem_all, vmem_all, p, l, seq)
        y_mla = _mla_attn_call(q, k, v, gate, batch, seq)
        y_diff = _diff_attn_call(dq, dk, dv, gate, lam_pad, subln_t, l, batch, seq, lam_init)
        x2d = _out_call(x2d, y_mla, y_diff, y_mem, w_out_b, l)
    return x2d.reshape(batch, seq, D_MODEL)
```

```python
import functools
import math

import numpy as np
import jax
import jax.numpy as jnp
from jax import lax
from jax.experimental import pallas as pl
from jax.experimental.pallas import tpu as pltpu

F32 = jnp.float32
BF16 = jnp.bfloat16

D_MODEL = 1024
ROPE_THETA = 10000.0
RMS_EPS = 1e-6
NEG_INF = -1e30
LOG2_E = math.log2(math.e)

MLA_HEADS = 8
Q_LORA = 256
KV_LORA = 128
MLA_NOPE = 64
MLA_ROPE = 32
MLA_QK = MLA_NOPE + MLA_ROPE
MLA_V = 64
MLA_W = MLA_HEADS * MLA_V

DIFF_HEADS = 4
DIFF_D = 32
DIFF_V = 2 * DIFF_D
DIFF_W = DIFF_HEADS * DIFF_V
DIFF_QK_W = 2 * DIFF_HEADS * DIFF_D

MEM_HEADS = 4
MEM_D = 64
MEM_W = MEM_HEADS * MEM_D

D_MIX = MLA_W + DIFF_W + MEM_W

LANES = 128
MXU_DIM = 256
ROPE_D = 32
HEAD_PAD = LANES

_C_CQ = 0
_C_CKV = _C_CQ + Q_LORA
_C_KR = _C_CKV + KV_LORA
_C_KR_SW = _C_KR + LANES
_C_DQ = _C_KR_SW + LANES
_C_DQ_SW = _C_DQ + DIFF_QK_W
_C_DK = _C_DQ_SW + DIFF_QK_W
_C_DK_SW = _C_DK + DIFF_QK_W
_C_DV = _C_DK_SW + DIFF_QK_W
_C_MQ = _C_DV + DIFF_HEADS * LANES
_C_Z = _C_MQ + MEM_W
_C_END = _C_Z + D_MIX

VMEM_LIMIT = 56 * 1024 * 1024

V_DIM = MLA_V
VT_ROWS = 80

TK = 512
TM_PROJ = TK
TM_OUT = 512
TQ_MLA = 512
TQ_DIFF = 512
SCORE_SLOTS = 2
TRIP_STEPS = 8
SCORE_CEIL = 3.0e38

_NT = (((1,), (1,)), ((), ()))


def _rms(x):
    return x * lax.rsqrt(jnp.mean(x * x, axis=-1, keepdims=True) + RMS_EPS)


def _lane_iota(shape):
    return lax.broadcasted_iota(jnp.int32, shape, len(shape) - 1)


def _seg_sum_sq(x, bd_ref):
    sq = x * x
    hi = sq.astype(BF16)
    lo = (sq - hi.astype(F32)).astype(BF16)
    bd = bd_ref[...]
    return jnp.dot(hi, bd, preferred_element_type=F32) + jnp.dot(lo, bd, preferred_element_type=F32)


def _rope_table_kernel(pos_ref, inv_ref, cos_ref, sin_ref):
    ang = pos_ref[...].astype(F32) * inv_ref[...]
    lane = _lane_iota(ang.shape)
    sign = jnp.where((lane & (ROPE_D // 2)) == 0, -1.0, 1.0)
    cos_ref[...] = jnp.cos(ang)
    sin_ref[...] = jnp.sin(ang) * sign


def _rope_tables(positions):
    n = positions.size
    tm = 1024
    inv = ROPE_THETA ** (-jnp.arange(0, ROPE_D, 2, dtype=F32) / ROPE_D)
    inv_t = jnp.tile(inv, LANES // (ROPE_D // 2))[None, :]
    pos = positions.reshape(n, 1)
    return pl.pallas_call(
        _rope_table_kernel,
        grid=(n // tm,),
        in_specs=[pl.BlockSpec((tm, 1), lambda i: (i, 0)),
                  pl.BlockSpec((1, LANES), lambda i: (0, 0))],
        out_specs=[pl.BlockSpec((tm, LANES), lambda i: (i, 0)),
                   pl.BlockSpec((tm, LANES), lambda i: (i, 0))],
        out_shape=[jax.ShapeDtypeStruct((n, LANES), F32)] * 2,
        name="rope_tables",
    )(pos, inv_t)


def _mem_kv_kernel(mem_ref, g_ref, w_ref, kn_ref, bd_ref, k_ref, v_ref):
    m = mem_ref[0]
    mn = (_rms(m) * g_ref[0]).astype(BF16)
    kv = jnp.dot(mn, w_ref[0], preferred_element_type=F32)
    k = kv[:, :MEM_W]
    k = k * lax.rsqrt(_seg_sum_sq(k, bd_ref) * (1.0 / MEM_D) + RMS_EPS) * kn_ref[0]
    k_ref[0, 0] = k.astype(BF16)
    v_ref[0, 0] = kv[:, MEM_W:].astype(BF16)


def _mem_kv(mem, mem_norm_g, w_mem_kv, mem_kn_g, bd_mem):
    depth = w_mem_kv.shape[0]
    b, m, d = mem.shape
    kn_t = jnp.tile(mem_kn_g, (1, MEM_HEADS))[:, None, :]
    return pl.pallas_call(
        _mem_kv_kernel,
        grid=(depth, b),
        in_specs=[pl.BlockSpec((1, m, d), lambda l, i: (i, 0, 0)),
                  pl.BlockSpec((1, 1, d), lambda l, i: (l, 0, 0)),
                  pl.BlockSpec((1, d, 2 * MEM_W), lambda l, i: (l, 0, 0)),
                  pl.BlockSpec((1, 1, MEM_W), lambda l, i: (l, 0, 0)),
                  pl.BlockSpec(bd_mem.shape, lambda l, i: (0, 0))],
        out_specs=[pl.BlockSpec((1, 1, m, MEM_W), lambda l, i: (l, i, 0, 0)),
                   pl.BlockSpec((1, 1, m, MEM_W), lambda l, i: (l, i, 0, 0))],
        out_shape=[jax.ShapeDtypeStruct((depth, b, m, MEM_W), BF16)] * 2,
        name="mem_kv",
    )(mem, mem_norm_g[:, None, :], w_mem_kv.astype(BF16), kn_t, bd_mem)


_PROJ_PARAMS = ("gx", "w_in", "gcq", "gckv", "w_uq", "w_uq_sw", "w_uk", "w_uv_t",
                "q_gc1", "q_gc0", "q_gs", "k_gc1", "k_gc0", "k_gs",
                "dq_gc", "dq_gs", "dk_gc", "dk_gs", "mqn", "bd_diff", "bd_mem")


def _proj_kernel(x_ref, cos_ref, sin_ref, gx_ref, win_ref, gcq_ref, gckv_ref, wuq_ref, wuqsw_ref, wuk_ref, wuvt_ref,
                 qgc1_ref, qgc0_ref, qgs_ref, kgc1_ref, kgc0_ref, kgs_ref,
                 dqgc_ref, dqgs_ref, dkgc_ref, dkgs_ref, mqn_ref, bdd_ref, bdm_ref, kmem_ref, vmem_ref,
                 q_out, k_out, v_out, dq_out, dk_out, dv_out, ymem_out, gate_out):
    tm = x_ref.shape[0]
    h = (_rms(x_ref[...]) * gx_ref[...]).astype(BF16)
    proj = jnp.dot(h, win_ref[...], preferred_element_type=F32)
    cos = cos_ref[...]
    sin = sin_ref[...]

    cqn = (_rms(proj[:, _C_CQ:_C_CQ + Q_LORA]) * gcq_ref[...]).astype(BF16)
    q = jnp.dot(cqn, wuq_ref[...], preferred_element_type=F32)
    q_sw = jnp.dot(cqn, wuqsw_ref[...], preferred_element_type=F32)
    q_gc = cos * qgc1_ref[...] + qgc0_ref[...]
    q_gs = sin * qgs_ref[...]
    for hd in range(MLA_HEADS):
        sl = slice(hd * HEAD_PAD, (hd + 1) * HEAD_PAD)
        qh = q[:, sl]
        r = lax.rsqrt(jnp.sum(qh * qh, axis=-1, keepdims=True) * (1.0 / MLA_QK) + RMS_EPS)
        q_out[:, sl] = ((qh * q_gc + q_sw[:, sl] * q_gs) * r).astype(BF16)

    ckv = _rms(proj[:, _C_CKV:_C_CKV + KV_LORA]) * gckv_ref[...]
    ckvn = ckv.astype(BF16)
    k_nope = jnp.dot(ckvn, wuk_ref[...], preferred_element_type=F32)
    kr = proj[:, _C_KR:_C_KR + LANES]
    k_gc = cos * kgc1_ref[...] + kgc0_ref[...]
    kr_rot = proj[:, _C_KR_SW:_C_KR_SW + LANES] * (sin * kgs_ref[...])
    for hd in range(MLA_HEADS):
        sl = slice(hd * HEAD_PAD, (hd + 1) * HEAD_PAD)
        kh = k_nope[:, sl] + kr
        r = lax.rsqrt(jnp.sum(kh * kh, axis=-1, keepdims=True) * (1.0 / MLA_QK) + RMS_EPS)
        k_out[:, sl] = ((kh * k_gc + kr_rot) * r).astype(BF16)
    vt = jnp.dot(wuvt_ref[...], ckv.T.astype(BF16), preferred_element_type=F32)
    ones_row = (lax.broadcasted_iota(jnp.int32, (VT_ROWS, tm), 0) == V_DIM).astype(F32)
    for hd in range(MLA_HEADS):
        rows = slice(hd * VT_ROWS, (hd + 1) * VT_ROWS)
        v_out[0, rows, :] = (vt[rows] + ones_row).astype(BF16)

    for raw_c, sw_c, gc_ref, gs_ref, out in ((_C_DQ, _C_DQ_SW, dqgc_ref, dqgs_ref, dq_out),
                                             (_C_DK, _C_DK_SW, dkgc_ref, dkgs_ref, dk_out)):
        raw = proj[:, raw_c:raw_c + DIFF_QK_W]
        r = lax.rsqrt(_seg_sum_sq(raw, bdd_ref) * (1.0 / DIFF_D) + RMS_EPS)
        for c in range(DIFF_QK_W // LANES):
            sl = slice(c * LANES, (c + 1) * LANES)
            rot = raw[:, sl] * (cos * gc_ref[:, sl]) + proj[:, sw_c + c * LANES:sw_c + (c + 1) * LANES] * (sin * gs_ref[:, sl])
            out[:, sl] = (rot * r[:, sl]).astype(BF16)
    ones_lane = (_lane_iota((1, LANES)) == V_DIM).astype(F32)
    for hd in range(DIFF_HEADS):
        dv_t = (proj[:, _C_DV + hd * LANES:_C_DV + (hd + 1) * LANES] + ones_lane).T
        dv_out[0, hd * VT_ROWS:(hd + 1) * VT_ROWS, :] = dv_t[:VT_ROWS].astype(BF16)

    z = proj[:, _C_Z:_C_END]
    gate = z * (1.0 / (1.0 + jnp.exp(-z)))
    gate_out[...] = gate[:, :MLA_W + DIFF_W].astype(BF16)

    mq = proj[:, _C_MQ:_C_MQ + MEM_W]
    mq = mq * lax.rsqrt(_seg_sum_sq(mq, bdm_ref) * (1.0 / MEM_D) + RMS_EPS) * mqn_ref[...]
    kmem = kmem_ref[0]
    vmem = vmem_ref[0]
    lane_w = _lane_iota((1, MEM_W))
    o = jnp.zeros(mq.shape, F32)
    for hd in range(MEM_HEADS):
        head_mask = (lane_w // MEM_D) == hd
        qh = jnp.where(head_mask, mq, 0.0).astype(BF16)
        s = lax.dot_general(qh, kmem, _NT, preferred_element_type=F32)
        p = jnp.exp(s - jnp.max(s, axis=-1, keepdims=True))
        inv_l = 1.0 / jnp.sum(p, axis=-1, keepdims=True)
        vh = jnp.where(head_mask, vmem, jnp.zeros_like(vmem))
        o = o + jnp.dot(p.astype(BF16), vh, preferred_element_type=F32) * inv_l
    ymem_out[...] = (o * gate[:, MLA_W + DIFF_W:]).astype(BF16)


def _proj_call(x2d, cos_t, sin_t, kmem, vmem, p, layer, seq):
    n = x2d.shape[0]
    tm = TM_PROJ
    per_batch = seq // tm
    row = lambda i: (i, 0)
    params = [p[name] for name in _PROJ_PARAMS]
    mem_spec = pl.BlockSpec((None, 1) + kmem.shape[2:], lambda i: (layer, i // per_batch, 0, 0))
    in_specs = ([pl.BlockSpec((tm, D_MODEL), row), pl.BlockSpec((tm, LANES), row), pl.BlockSpec((tm, LANES), row)]
                + [_layer_spec(a, layer) for a in params] + [mem_spec, mem_spec])

    def rows_out(width):
        return pl.BlockSpec((tm, width), row), jax.ShapeDtypeStruct((n, width), BF16)

    def values_t_out(heads):
        return (pl.BlockSpec((1, heads * VT_ROWS, tm), lambda i: (i, 0, 0)),
                jax.ShapeDtypeStruct((n // tm, heads * VT_ROWS, tm), BF16))

    outs = [rows_out(MLA_HEADS * HEAD_PAD), rows_out(MLA_HEADS * HEAD_PAD), values_t_out(MLA_HEADS),
            rows_out(DIFF_QK_W), rows_out(DIFF_QK_W), values_t_out(DIFF_HEADS),
            rows_out(MEM_W), rows_out(MLA_W + DIFF_W)]
    return pl.pallas_call(
        _proj_kernel,
        grid=(n // tm,),
        in_specs=in_specs,
        out_specs=[o[0] for o in outs],
        out_shape=[o[1] for o in outs],
        compiler_params=pltpu.CompilerParams(dimension_semantics=("arbitrary",), vmem_limit_bytes=VMEM_LIMIT),
        name="proj_prep",
    )(x2d, cos_t, sin_t, *params, kmem, vmem)


def _causal_keep_t(tk, tq, key_offset):
    r = lax.broadcasted_iota(jnp.int32, (tk, tq), 0)
    c = lax.broadcasted_iota(jnp.int32, (tk, tq), 1)
    return (r + key_offset) <= c


def _init_limits(lim_ref, tk):
    for c in range(lim_ref.shape[1]):
        lim_ref[0, c] = jnp.full((tk, MXU_DIM), SCORE_CEIL, F32)
        lim_ref[1, c] = jnp.where(_causal_keep_t(tk, MXU_DIM, -c * MXU_DIM), SCORE_CEIL, NEG_INF)


def _score_strip(k_blk, q_strip, lim_ref, diag, slot, idx, c):
    s_ref, mb_ref = slot
    s = lax.dot_general(k_blk, q_strip, _NT, preferred_element_type=F32)
    if diag is not None:
        s = jnp.minimum(s, lim_ref[diag, c])
    s_ref[idx, c] = s
    mb_ref[idx, c] = jnp.max(s, axis=0, keepdims=True)


def _softmax_pv_strip(slot, idx, c, vt_blk, first, m_ref, acc_ref, qi):
    s_ref, mb_ref = slot
    s = s_ref[idx, c]
    m_old = jnp.where(first, NEG_INF, m_ref[qi, idx, c])
    m_new = jnp.maximum(m_old, mb_ref[idx, c])
    alpha = jnp.exp2(m_old - m_new)
    p = jnp.exp2(s - m_new).astype(BF16)
    acc_ref[qi, idx, c] = acc_ref[qi, idx, c] * alpha + jnp.dot(vt_blk, p, preferred_element_type=F32)
    m_ref[qi, idx, c] = m_new


def _causal_steps(nq):
    below = [(i, j) for i in range(nq) for j in range(i)]
    qi = np.array([i for i in range(nq)] + [i for i, _ in below], np.int32)
    kj = np.array([i for i in range(nq)] + [j for _, j in below], np.int32)
    first = (np.arange(len(qi)) < nq).astype(np.int32)
    trips = (len(qi) - 1) // TRIP_STEPS
    on_diag = (qi == kj)[1:1 + trips * TRIP_STEPS].reshape(trips, TRIP_STEPS)
    return qi, kj, first, np.append(on_diag.any(axis=1), False).astype(np.int32)


def _flat_sweep(n_steps, n_diag, trip_diag, scores, update):
    def pair(t0, u, with_scores, masked):
        ahead = scores(t0 + u + 1, (u + 1) % SCORE_SLOTS, masked) if with_scores else []
        now = update(t0 + u, u % SCORE_SLOTS)
        for i in range(max(len(ahead), len(now))):
            if i < len(ahead):
                ahead[i]()
            if i < len(now):
                now[i]()

    for piece in scores(0, 0, True):
        piece()
    trips = (n_steps - 1) // TRIP_STEPS

    def trip(i, masked):
        for u in range(TRIP_STEPS):
            pair(i * TRIP_STEPS, u, True, masked)

    def body(i, carry):
        lax.cond(trip_diag[i] != 0, functools.partial(trip, i, True), functools.partial(trip, i, False))
        return carry

    lax.fori_loop(0, trips, body, 0)
    done = trips * TRIP_STEPS
    for u in range(n_steps - done):
        pair(done, u, done + u + 1 < n_steps, done + u + 1 < n_diag)


def _normalised_t(acc_ref, qb, idx):
    strips = []
    for c in range(acc_ref.shape[2]):
        acc = acc_ref[qb, idx, c]
        strips.append(acc[:V_DIM] * (1.0 / acc[V_DIM:V_DIM + 1]))
    return jnp.concatenate(strips, axis=1)


def _attn_scratch(count, nq, tk, tq):
    ns = tq // MXU_DIM
    shapes = []
    for _ in range(SCORE_SLOTS):
        shapes += [pltpu.VMEM((count, ns, tk, MXU_DIM), F32), pltpu.VMEM((count, ns, 1, MXU_DIM), F32)]
    return shapes + [pltpu.VMEM((2, ns, tk, MXU_DIM), F32), pltpu.VMEM((nq, count, ns, 1, MXU_DIM), F32),
                     pltpu.VMEM((nq, count, ns, VT_ROWS, MXU_DIM), F32)]


def _split_scratch(scratch, tk):
    slots = [(scratch[2 * i], scratch[2 * i + 1]) for i in range(SCORE_SLOTS)]
    lim_ref, m_ref, acc_ref = scratch[2 * SCORE_SLOTS:]
    _init_limits(lim_ref, tk)
    m_ref[...] = jnp.full(m_ref.shape, NEG_INF, F32)
    acc_ref[...] = jnp.zeros(acc_ref.shape, F32)
    return slots, lim_ref, m_ref, acc_ref


def _mla_attn_kernel(qi_tab, kj_tab, first_tab, trip_diag, q0_ref, q1_ref, k0_ref, k1_ref, vt_ref, gate_ref, o_ref,
                     *scratch, tq, tk, nq):
    slots, lim_ref, m_ref, acc_ref = _split_scratch(scratch, tk)
    q_refs, k_refs = (q0_ref, q1_ref), (k0_ref, k1_ref)
    n_strips = tq // MXU_DIM

    def scores(t, slot, masked):
        qi, kj = qi_tab[t], kj_tab[t]
        krows = pl.ds(pl.multiple_of(kj * tk, tk), tk)
        diag = (kj == qi).astype(jnp.int32) if masked else None

        def piece(hd, c):
            qrows = pl.ds(pl.multiple_of(qi * tq + c * MXU_DIM, MXU_DIM), MXU_DIM)
            _score_strip(k_refs[hd][krows, :], q_refs[hd][qrows, :], lim_ref, diag, slots[slot], hd, c)

        return [functools.partial(piece, hd, c) for hd in range(2) for c in range(n_strips)]

    def update(t, slot):
        qi, kj = qi_tab[t], kj_tab[t]

        def piece(hd, c):
            _softmax_pv_strip(slots[slot], hd, c, vt_ref[kj, hd * VT_ROWS:(hd + 1) * VT_ROWS, :],
                              first_tab[t] != 0, m_ref, acc_ref, qi)

        return [functools.partial(piece, hd, c) for hd in range(2) for c in range(n_strips)]

    _flat_sweep(nq * (nq + 1) // 2, nq, trip_diag, scores, update)

    def finish(qb, carry):
        rows = pl.ds(pl.multiple_of(qb * tq, tq), tq)
        o_t = jnp.concatenate([_normalised_t(acc_ref, qb, 0), _normalised_t(acc_ref, qb, 1)], axis=0)
        o_ref[rows, :] = (o_t.T * gate_ref[rows, :].astype(F32)).astype(BF16)
        return carry

    lax.fori_loop(0, nq, finish, 0)


def _mla_attn_call(q, k, vt, gate, batch, seq):
    tq, tk = TQ_MLA, TK
    nq = seq // tq
    pairs = MLA_HEADS // 2
    step_tabs = _causal_steps(nq)
    blk = lambda b, hp, *_: (b, hp)
    head0 = pl.BlockSpec((seq, HEAD_PAD), lambda b, hp, *_: (b, 2 * hp))
    head1 = pl.BlockSpec((seq, HEAD_PAD), lambda b, hp, *_: (b, 2 * hp + 1))
    grid_spec = pltpu.PrefetchScalarGridSpec(
        num_scalar_prefetch=len(step_tabs),
        grid=(batch, pairs),
        in_specs=[head0, head1, head0, head1,
                  pl.BlockSpec((seq // tk, 2 * VT_ROWS, tk), lambda b, hp, *_: (b, hp, 0)),
                  pl.BlockSpec((seq, LANES), blk)],
        out_specs=pl.BlockSpec((seq, LANES), blk),
        scratch_shapes=_attn_scratch(2, nq, tk, tq))
    return pl.pallas_call(
        functools.partial(_mla_attn_kernel, tq=tq, tk=tk, nq=nq),
        grid_spec=grid_spec,
        out_shape=jax.ShapeDtypeStruct((batch * seq, MLA_W), BF16),
        compiler_params=pltpu.CompilerParams(dimension_semantics=("arbitrary",) * 2, vmem_limit_bytes=VMEM_LIMIT),
        name="mla_attn",
    )(*step_tabs, q, q, k, k, vt, gate)


def _diff_attn_kernel(qi_tab, kj_tab, first_tab, trip_diag, q_ref, k_ref, vt_ref, gate_ref, lam_ref, sg_ref, o_ref,
                      *scratch, tq, tk, nq, lam_init):
    slots, lim_ref, m_ref, acc_ref = _split_scratch(scratch, tk)
    n_strips = tq // MXU_DIM
    map_of_lane = _lane_iota((1, LANES)) // DIFF_D

    def scores(t, slot, masked):
        qi, kj = qi_tab[t], kj_tab[t]
        krows = pl.ds(pl.multiple_of(kj * tk, tk), tk)
        diag = (kj == qi).astype(jnp.int32) if masked else None

        def piece(idx, c):
            q_strip = q_ref[pl.ds(pl.multiple_of(qi * tq + c * MXU_DIM, MXU_DIM), MXU_DIM), :]
            q_map = jnp.where(map_of_lane == idx, q_strip, jnp.zeros_like(q_strip))
            _score_strip(k_ref[krows, :], q_map, lim_ref, diag, slots[slot], idx, c)

        return [functools.partial(piece, idx, c) for idx in range(4) for c in range(n_strips)]

    def update(t, slot):
        qi, kj = qi_tab[t], kj_tab[t]

        def piece(idx, c):
            hd = idx // 2
            _softmax_pv_strip(slots[slot], idx, c, vt_ref[kj, hd * VT_ROWS:(hd + 1) * VT_ROWS, :],
                              first_tab[t] != 0, m_ref, acc_ref, qi)

        return [functools.partial(piece, idx, c) for idx in range(4) for c in range(n_strips)]

    _flat_sweep(nq * (nq + 1) // 2, nq, trip_diag, scores, update)

    lv = lam_ref[...]
    lam = (jnp.exp(jnp.sum(lv[0:1] * lv[1:2], axis=-1, keepdims=True))
           - jnp.exp(jnp.sum(lv[2:3] * lv[3:4], axis=-1, keepdims=True)) + lam_init)

    def finish(qb, carry):
        rows = pl.ds(pl.multiple_of(qb * tq, tq), tq)
        heads_t = []
        for hd in range(2):
            o = _normalised_t(acc_ref, qb, 2 * hd) - lam * _normalised_t(acc_ref, qb, 2 * hd + 1)
            heads_t.append(o * lax.rsqrt(jnp.mean(o * o, axis=0, keepdims=True) + RMS_EPS))
        o = jnp.concatenate(heads_t, axis=0).T * sg_ref[...] * (1.0 - lam_init)
        o_ref[rows, :] = (o * gate_ref[rows, :].astype(F32)).astype(BF16)
        return carry

    lax.fori_loop(0, nq, finish, 0)


def _diff_attn_call(q, k, vt, gate, lam_pad, subln_t, layer, batch, seq, lam_init):
    tq, tk = TQ_DIFF, TK
    nq = seq // tq
    pairs = DIFF_HEADS // 2
    gate_off = MLA_W // LANES
    step_tabs = _causal_steps(nq)
    blk = lambda b, hp, *_: (b, hp)
    grid_spec = pltpu.PrefetchScalarGridSpec(
        num_scalar_prefetch=len(step_tabs),
        grid=(batch, pairs),
        in_specs=[pl.BlockSpec((seq, LANES), blk),
                  pl.BlockSpec((seq, LANES), blk),
                  pl.BlockSpec((seq // tk, 2 * VT_ROWS, tk), lambda b, hp, *_: (b, hp, 0)),
                  pl.BlockSpec((seq, LANES), lambda b, hp, *_: (b, gate_off + hp)),
                  _layer_spec(lam_pad, layer),
                  _layer_spec(subln_t, layer)],
        out_specs=pl.BlockSpec((seq, LANES), blk),
        scratch_shapes=_attn_scratch(4, nq, tk, tq))
    return pl.pallas_call(
        functools.partial(_diff_attn_kernel, tq=tq, tk=tk, nq=nq, lam_init=lam_init),
        grid_spec=grid_spec,
        out_shape=jax.ShapeDtypeStruct((batch * seq, DIFF_W), BF16),
        compiler_params=pltpu.CompilerParams(dimension_semantics=("arbitrary",) * 2, vmem_limit_bytes=VMEM_LIMIT),
        name="diff_attn",
    )(*step_tabs, q, k, vt, gate, lam_pad, subln_t)


def _out_kernel(x_ref, ymla_ref, ydiff_ref, ymem_ref, w_ref, o_ref):
    y = jnp.concatenate([ymla_ref[...], ydiff_ref[...], ymem_ref[...]], axis=1)
    o_ref[...] = x_ref[...] + jnp.dot(y, w_ref[...], preferred_element_type=F32)


def _out_call(x2d, y_mla, y_diff, y_mem, w_out, layer):
    n = x2d.shape[0]
    tm = TM_OUT
    row = lambda i: (i, 0)
    return pl.pallas_call(
        _out_kernel,
        grid=(n // tm,),
        in_specs=[pl.BlockSpec((tm, D_MODEL), row), pl.BlockSpec((tm, MLA_W), row),
                  pl.BlockSpec((tm, DIFF_W), row), pl.BlockSpec((tm, MEM_W), row),
                  _layer_spec(w_out, layer)],
        out_specs=pl.BlockSpec((tm, D_MODEL), row),
        out_shape=jax.ShapeDtypeStruct((n, D_MODEL), F32),
        compiler_params=pltpu.CompilerParams(dimension_semantics=("arbitrary",), vmem_limit_bytes=VMEM_LIMIT),
        name="out_proj",
    )(x2d, y_mla, y_diff, y_mem, w_out)


def _block_diag_ones(seg):
    idx = np.arange(MXU_DIM) // seg
    return jnp.asarray(idx[:, None] == idx[None, :], dtype=BF16)


def _pair_swapped(a):
    lead = a.shape[:-1]
    a4 = a.reshape(lead + (a.shape[-1] // ROPE_D, 2, ROPE_D // 2))
    return a4[..., ::-1, :].reshape(a.shape)


def _head_padded(a, heads, width, offset=0):
    lead = a.shape[:-1]
    a3 = a.reshape(lead + (heads, width))
    pad = [(0, 0)] * (a3.ndim - 1) + [(offset, LANES - width - offset)]
    return jnp.pad(a3, pad).reshape(lead + (heads * LANES,))


def _layer_spec(a, layer):
    zeros = (0,) * (a.ndim - 1)
    return pl.BlockSpec((None,) + a.shape[1:], lambda *_: (layer,) + zeros)


def _stacked_params(norm_g, w_in, mla_q_norm_g, mla_kv_norm_g, w_uq, w_ukv, mla_qn_g, mla_kn_g,
                    diff_qn_g, diff_kn_g, mem_qn_g):
    depth = w_in.shape[0]
    o_cq, o_ckv, o_kr = 0, Q_LORA, Q_LORA + KV_LORA
    o_dq = o_kr + MLA_ROPE
    o_dk = o_dq + DIFF_QK_W
    o_dv = o_dk + DIFF_QK_W
    o_mq = o_dv + DIFF_W
    o_z = o_mq + MEM_W
    w = w_in
    w_kr, w_dq, w_dk = w[..., o_kr:o_dq], w[..., o_dq:o_dk], w[..., o_dk:o_dv]
    w_in_p = jnp.concatenate([
        w[..., o_cq:o_ckv], w[..., o_ckv:o_kr],
        _head_padded(w_kr, 1, MLA_ROPE, MLA_NOPE), _head_padded(_pair_swapped(w_kr), 1, MLA_ROPE, MLA_NOPE),
        w_dq, _pair_swapped(w_dq), w_dk, _pair_swapped(w_dk),
        _head_padded(w[..., o_dv:o_mq], DIFF_HEADS, DIFF_V),
        w[..., o_mq:o_z], w[..., o_z:]], axis=-1).astype(BF16)

    uq4 = w_uq.reshape(depth, Q_LORA, MLA_HEADS, MLA_QK)
    w_uq_p = _head_padded(w_uq, MLA_HEADS, MLA_QK).astype(BF16)
    w_uq_sw = _head_padded(_pair_swapped(uq4[..., MLA_NOPE:]).reshape(depth, Q_LORA, MLA_HEADS * MLA_ROPE),
                           MLA_HEADS, MLA_ROPE, MLA_NOPE).astype(BF16)
    kv4 = w_ukv.reshape(depth, KV_LORA, MLA_HEADS, MLA_NOPE + MLA_V)
    w_uk_p = _head_padded(kv4[..., :MLA_NOPE].reshape(depth, KV_LORA, MLA_HEADS * MLA_NOPE),
                          MLA_HEADS, MLA_NOPE).astype(BF16)
    w_uv_t = jnp.pad(jnp.transpose(kv4[..., MLA_NOPE:], (0, 2, 3, 1)),
                     ((0, 0), (0, 0), (0, VT_ROWS - MLA_V), (0, 0))
                     ).reshape(depth, MLA_HEADS * VT_ROWS, KV_LORA).astype(BF16)

    q_scale = LOG2_E / math.sqrt(MLA_QK)
    d_scale = LOG2_E / math.sqrt(DIFF_D)
    row = lambda v: v[:, None, :]
    rope_vec = lambda g: row(jnp.pad(g[:, MLA_NOPE:], ((0, 0), (MLA_NOPE, LANES - MLA_QK))))
    rope_vec_sw = lambda g: row(jnp.pad(_pair_swapped(g[:, MLA_NOPE:]), ((0, 0), (MLA_NOPE, LANES - MLA_QK))))
    nope_vec = lambda g: row(jnp.pad(g[:, :MLA_NOPE], ((0, 0), (0, LANES - MLA_NOPE))))
    dqn = row(jnp.tile(diff_qn_g, (1, 2 * DIFF_HEADS)))
    dkn = row(jnp.tile(diff_kn_g, (1, 2 * DIFF_HEADS)))
    return {
        "gx": row(norm_g),
        "w_in": w_in_p,
        "gcq": row(mla_q_norm_g),
        "gckv": row(mla_kv_norm_g),
        "w_uq": w_uq_p, "w_uq_sw": w_uq_sw, "w_uk": w_uk_p, "w_uv_t": w_uv_t,
        "q_gc1": rope_vec(mla_qn_g) * q_scale, "q_gc0": nope_vec(mla_qn_g) * q_scale,
        "q_gs": rope_vec_sw(mla_qn_g) * q_scale,
        "k_gc1": rope_vec(mla_kn_g), "k_gc0": nope_vec(mla_kn_g), "k_gs": rope_vec_sw(mla_kn_g),
        "dq_gc": dqn * d_scale, "dq_gs": _pair_swapped(dqn) * d_scale,
        "dk_gc": dkn, "dk_gs": _pair_swapped(dkn),
        "mqn": row(jnp.tile(mem_qn_g, (1, MEM_HEADS))) * (1.0 / math.sqrt(MEM_D)),
        "bd_diff": jnp.broadcast_to(_block_diag_ones(DIFF_D), (depth, MXU_DIM, MXU_DIM)),
        "bd_mem": jnp.broadcast_to(_block_diag_ones(MEM_D), (depth, MXU_DIM, MXU_DIM)),
    }


def kernel(x, mem, positions, norm_g, w_in, mla_q_norm_g, mla_kv_norm_g, w_uq, w_ukv, mla_qn_g, mla_kn_g,
           diff_qn_g, diff_kn_g, diff_lambda, diff_subln_g, mem_norm_g, w_mem_kv, mem_qn_g, mem_kn_g, w_out):
    batch, seq, d_model = x.shape
    depth = w_in.shape[0]
    assert d_model == D_MODEL and MLA_V == DIFF_V and TQ_MLA == TK and TQ_DIFF == TK
    assert TRIP_STEPS % SCORE_SLOTS == 0 and SCORE_SLOTS >= 2 and TK % MXU_DIM == 0
    assert seq % max(TM_PROJ, TQ_MLA, TQ_DIFF, TM_OUT) == 0 and (batch * seq) % 1024 == 0
    cos_t, sin_t = _rope_tables(positions)
    kmem_all, vmem_all = _mem_kv(mem, mem_norm_g, w_mem_kv, mem_kn_g, _block_diag_ones(MEM_D))
    p = _stacked_params(norm_g, w_in, mla_q_norm_g, mla_kv_norm_g, w_uq, w_ukv, mla_qn_g, mla_kn_g,
                        diff_qn_g, diff_kn_g, mem_qn_g)
    lam_pad = jnp.pad(diff_lambda, ((0, 0), (0, 4), (0, LANES - DIFF_D)))
    subln_t = jnp.tile(diff_subln_g, (1, 2))[:, None, :]
    w_out_b = w_out.astype(BF16)
    x2d = x.reshape(batch * seq, D_MODEL)
    for l in range(depth):
        lam_init = 0.8 - 0.6 * math.exp(-0.3 * l)
        q, k, v, dq, dk, dv, y_mem, gate = _proj_call(x2d, cos_t, sin_t, kmem_all, vmem_all, p, l, seq)
        y_mla = _mla_attn_call(q, k, v, gate, batch, seq)
        y_diff = _diff_attn_call(dq, dk, dv, gate, lam_pad, subln_t, l, batch, seq, lam_init)
        x2d = _out_call(x2d, y_mla, y_diff, y_mem, w_out_b, l)
    return x2d.reshape(batch, seq, D_MODEL)
```

```python
import functools
import math

import numpy as np
import jax
import jax.numpy as jnp
from jax import lax
from jax.experimental import pallas as pl
from jax.experimental.pallas import tpu as pltpu

F32 = jnp.float32
BF16 = jnp.bfloat16

D_MODEL = 1024
ROPE_THETA = 10000.0
RMS_EPS = 1e-6
NEG_INF = -1e30
LOG2_E = math.log2(math.e)

MLA_HEADS = 8
Q_LORA = 256
KV_LORA = 128
MLA_NOPE = 64
MLA_ROPE = 32
MLA_QK = MLA_NOPE + MLA_ROPE
MLA_V = 64
MLA_W = MLA_HEADS * MLA_V

DIFF_HEADS = 4
DIFF_D = 32
DIFF_V = 2 * DIFF_D
DIFF_W = DIFF_HEADS * DIFF_V
DIFF_QK_W = 2 * DIFF_HEADS * DIFF_D

MEM_HEADS = 4
MEM_D = 64
MEM_W = MEM_HEADS * MEM_D

D_MIX = MLA_W + DIFF_W + MEM_W

LANES = 128
MXU_DIM = 256
ROPE_D = 32
HEAD_PAD = LANES

_C_CQ = 0
_C_CKV = _C_CQ + Q_LORA
_C_KR = _C_CKV + KV_LORA
_C_KR_SW = _C_KR + LANES
_C_DQ = _C_KR_SW + LANES
_C_DQ_SW = _C_DQ + DIFF_QK_W
_C_DK = _C_DQ_SW + DIFF_QK_W
_C_DK_SW = _C_DK + DIFF_QK_W
_C_DV = _C_DK_SW + DIFF_QK_W
_C_MQ = _C_DV + DIFF_HEADS * LANES
_C_Z = _C_MQ + MEM_W
_C_END = _C_Z + D_MIX

VMEM_LIMIT = 56 * 1024 * 1024

V_DIM = MLA_V
VT_ROWS = 80

TK = 512
TM_PROJ = TK
TM_OUT = 512
TQ_MLA = 512
TQ_DIFF = 512
SCORE_SLOTS = 2
TRIP_STEPS = 16
SCORE_CEIL = 3.0e38

_NT = (((1,), (1,)), ((), ()))


def _rms(x):
    return x * lax.rsqrt(jnp.mean(x * x, axis=-1, keepdims=True) + RMS_EPS)


def _lane_iota(shape):
    return lax.broadcasted_iota(jnp.int32, shape, len(shape) - 1)


def _seg_sum_sq(x, bd_ref):
    sq = x * x
    hi = sq.astype(BF16)
    lo = (sq - hi.astype(F32)).astype(BF16)
    bd = bd_ref[...]
    return jnp.dot(hi, bd, preferred_element_type=F32) + jnp.dot(lo, bd, preferred_element_type=F32)


def _rope_table_kernel(pos_ref, inv_ref, cos_ref, sin_ref):
    ang = pos_ref[...].astype(F32) * inv_ref[...]
    lane = _lane_iota(ang.shape)
    sign = jnp.where((lane & (ROPE_D // 2)) == 0, -1.0, 1.0)
    cos_ref[...] = jnp.cos(ang)
    sin_ref[...] = jnp.sin(ang) * sign


def _rope_tables(positions):
    n = positions.size
    tm = 1024
    inv = ROPE_THETA ** (-jnp.arange(0, ROPE_D, 2, dtype=F32) / ROPE_D)
    inv_t = jnp.tile(inv, LANES // (ROPE_D // 2))[None, :]
    pos = positions.reshape(n, 1)
    return pl.pallas_call(
        _rope_table_kernel,
        grid=(n // tm,),
        in_specs=[pl.BlockSpec((tm, 1), lambda i: (i, 0)),
                  pl.BlockSpec((1, LANES), lambda i: (0, 0))],
        out_specs=[pl.BlockSpec((tm, LANES), lambda i: (i, 0)),
                   pl.BlockSpec((tm, LANES), lambda i: (i, 0))],
        out_shape=[jax.ShapeDtypeStruct((n, LANES), F32)] * 2,
        name="rope_tables",
    )(pos, inv_t)


def _mem_kv_kernel(mem_ref, g_ref, w_ref, kn_ref, bd_ref, k_ref, v_ref):
    m = mem_ref[0]
    mn = (_rms(m) * g_ref[0]).astype(BF16)
    kv = jnp.dot(mn, w_ref[0], preferred_element_type=F32)
    k = kv[:, :MEM_W]
    k = k * lax.rsqrt(_seg_sum_sq(k, bd_ref) * (1.0 / MEM_D) + RMS_EPS) * kn_ref[0]
    k_ref[0, 0] = k.astype(BF16)
    v_ref[0, 0] = kv[:, MEM_W:].astype(BF16)


def _mem_kv(mem, mem_norm_g, w_mem_kv, mem_kn_g, bd_mem):
    depth = w_mem_kv.shape[0]
    b, m, d = mem.shape
    kn_t = jnp.tile(mem_kn_g, (1, MEM_HEADS))[:, None, :]
    return pl.pallas_call(
        _mem_kv_kernel,
        grid=(depth, b),
        in_specs=[pl.BlockSpec((1, m, d), lambda l, i: (i, 0, 0)),
                  pl.BlockSpec((1, 1, d), lambda l, i: (l, 0, 0)),
                  pl.BlockSpec((1, d, 2 * MEM_W), lambda l, i: (l, 0, 0)),
                  pl.BlockSpec((1, 1, MEM_W), lambda l, i: (l, 0, 0)),
                  pl.BlockSpec(bd_mem.shape, lambda l, i: (0, 0))],
        out_specs=[pl.BlockSpec((1, 1, m, MEM_W), lambda l, i: (l, i, 0, 0)),
                   pl.BlockSpec((1, 1, m, MEM_W), lambda l, i: (l, i, 0, 0))],
        out_shape=[jax.ShapeDtypeStruct((depth, b, m, MEM_W), BF16)] * 2,
        name="mem_kv",
    )(mem, mem_norm_g[:, None, :], w_mem_kv.astype(BF16), kn_t, bd_mem)


_PROJ_PARAMS = ("gx", "w_in", "gcq", "gckv", "w_uq", "w_uq_sw", "w_uk", "w_uv_t",
                "q_gc1", "q_gc0", "q_gs", "k_gc1", "k_gc0", "k_gs",
                "dq_gc", "dq_gs", "dk_gc", "dk_gs", "mqn", "bd_diff", "bd_mem")


def _proj_kernel(x_ref, cos_ref, sin_ref, gx_ref, win_ref, gcq_ref, gckv_ref, wuq_ref, wuqsw_ref, wuk_ref, wuvt_ref,
                 qgc1_ref, qgc0_ref, qgs_ref, kgc1_ref, kgc0_ref, kgs_ref,
                 dqgc_ref, dqgs_ref, dkgc_ref, dkgs_ref, mqn_ref, bdd_ref, bdm_ref, kmem_ref, vmem_ref,
                 q_out, k_out, v_out, dq_out, dk_out, dv_out, ymem_out, gate_out):
    tm = x_ref.shape[0]
    h = (_rms(x_ref[...]) * gx_ref[...]).astype(BF16)
    proj = jnp.dot(h, win_ref[...], preferred_element_type=F32)
    cos = cos_ref[...]
    sin = sin_ref[...]

    cqn = (_rms(proj[:, _C_CQ:_C_CQ + Q_LORA]) * gcq_ref[...]).astype(BF16)
    q = jnp.dot(cqn, wuq_ref[...], preferred_element_type=F32)
    q_sw = jnp.dot(cqn, wuqsw_ref[...], preferred_element_type=F32)
    q_gc = cos * qgc1_ref[...] + qgc0_ref[...]
    q_gs = sin * qgs_ref[...]
    for hd in range(MLA_HEADS):
        sl = slice(hd * HEAD_PAD, (hd + 1) * HEAD_PAD)
        qh = q[:, sl]
        r = lax.rsqrt(jnp.sum(qh * qh, axis=-1, keepdims=True) * (1.0 / MLA_QK) + RMS_EPS)
        q_out[:, sl] = ((qh * q_gc + q_sw[:, sl] * q_gs) * r).astype(BF16)

    ckv = _rms(proj[:, _C_CKV:_C_CKV + KV_LORA]) * gckv_ref[...]
    ckvn = ckv.astype(BF16)
    k_nope = jnp.dot(ckvn, wuk_ref[...], preferred_element_type=F32)
    kr = proj[:, _C_KR:_C_KR + LANES]
    k_gc = cos * kgc1_ref[...] + kgc0_ref[...]
    kr_rot = proj[:, _C_KR_SW:_C_KR_SW + LANES] * (sin * kgs_ref[...])
    for hd in range(MLA_HEADS):
        sl = slice(hd * HEAD_PAD, (hd + 1) * HEAD_PAD)
        kh = k_nope[:, sl] + kr
        r = lax.rsqrt(jnp.sum(kh * kh, axis=-1, keepdims=True) * (1.0 / MLA_QK) + RMS_EPS)
        k_out[:, sl] = ((kh * k_gc + kr_rot) * r).astype(BF16)
    vt = jnp.dot(wuvt_ref[...], ckv.T.astype(BF16), preferred_element_type=F32)
    ones_row = (lax.broadcasted_iota(jnp.int32, (VT_ROWS, tm), 0) == V_DIM).astype(F32)
    for hd in range(MLA_HEADS):
        rows = slice(hd * VT_ROWS, (hd + 1) * VT_ROWS)
        v_out[0, rows, :] = (vt[rows] + ones_row).astype(BF16)

    for raw_c, sw_c, gc_ref, gs_ref, out in ((_C_DQ, _C_DQ_SW, dqgc_ref, dqgs_ref, dq_out),
                                             (_C_DK, _C_DK_SW, dkgc_ref, dkgs_ref, dk_out)):
        raw = proj[:, raw_c:raw_c + DIFF_QK_W]
        r = lax.rsqrt(_seg_sum_sq(raw, bdd_ref) * (1.0 / DIFF_D) + RMS_EPS)
        for c in range(DIFF_QK_W // LANES):
            sl = slice(c * LANES, (c + 1) * LANES)
            rot = raw[:, sl] * (cos * gc_ref[:, sl]) + proj[:, sw_c + c * LANES:sw_c + (c + 1) * LANES] * (sin * gs_ref[:, sl])
            out[:, sl] = (rot * r[:, sl]).astype(BF16)
    ones_lane = (_lane_iota((1, LANES)) == V_DIM).astype(F32)
    for hd in range(DIFF_HEADS):
        dv_t = (proj[:, _C_DV + hd * LANES:_C_DV + (hd + 1) * LANES] + ones_lane).T
        dv_out[0, hd * VT_ROWS:(hd + 1) * VT_ROWS, :] = dv_t[:VT_ROWS].astype(BF16)

    z = proj[:, _C_Z:_C_END]
    gate = z * (1.0 / (1.0 + jnp.exp(-z)))
    gate_out[...] = gate[:, :MLA_W + DIFF_W].astype(BF16)

    mq = proj[:, _C_MQ:_C_MQ + MEM_W]
    mq = mq * lax.rsqrt(_seg_sum_sq(mq, bdm_ref) * (1.0 / MEM_D) + RMS_EPS) * mqn_ref[...]
    kmem = kmem_ref[0]
    vmem = vmem_ref[0]
    lane_w = _lane_iota((1, MEM_W))
    o = jnp.zeros(mq.shape, F32)
    for hd in range(MEM_HEADS):
        head_mask = (lane_w // MEM_D) == hd
        qh = jnp.where(head_mask, mq, 0.0).astype(BF16)
        s = lax.dot_general(qh, kmem, _NT, preferred_element_type=F32)
        p = jnp.exp(s - jnp.max(s, axis=-1, keepdims=True))
        inv_l = 1.0 / jnp.sum(p, axis=-1, keepdims=True)
        vh = jnp.where(head_mask, vmem, jnp.zeros_like(vmem))
        o = o + jnp.dot(p.astype(BF16), vh, preferred_element_type=F32) * inv_l
    ymem_out[...] = (o * gate[:, MLA_W + DIFF_W:]).astype(BF16)


def _proj_call(x2d, cos_t, sin_t, kmem, vmem, p, layer, seq):
    n = x2d.shape[0]
    tm = TM_PROJ
    per_batch = seq // tm
    row = lambda i: (i, 0)
    params = [p[name] for name in _PROJ_PARAMS]
    mem_spec = pl.BlockSpec((None, 1) + kmem.shape[2:], lambda i: (layer, i // per_batch, 0, 0))
    in_specs = ([pl.BlockSpec((tm, D_MODEL), row), pl.BlockSpec((tm, LANES), row), pl.BlockSpec((tm, LANES), row)]
                + [_layer_spec(a, layer) for a in params] + [mem_spec, mem_spec])

    def rows_out(width):
        return pl.BlockSpec((tm, width), row), jax.ShapeDtypeStruct((n, width), BF16)

    def values_t_out(heads):
        return (pl.BlockSpec((1, heads * VT_ROWS, tm), lambda i: (i, 0, 0)),
                jax.ShapeDtypeStruct((n // tm, heads * VT_ROWS, tm), BF16))

    outs = [rows_out(MLA_HEADS * HEAD_PAD), rows_out(MLA_HEADS * HEAD_PAD), values_t_out(MLA_HEADS),
            rows_out(DIFF_QK_W), rows_out(DIFF_QK_W), values_t_out(DIFF_HEADS),
            rows_out(MEM_W), rows_out(MLA_W + DIFF_W)]
    return pl.pallas_call(
        _proj_kernel,
        grid=(n // tm,),
        in_specs=in_specs,
        out_specs=[o[0] for o in outs],
        out_shape=[o[1] for o in outs],
        compiler_params=pltpu.CompilerParams(dimension_semantics=("arbitrary",), vmem_limit_bytes=VMEM_LIMIT),
        name="proj_prep",
    )(x2d, cos_t, sin_t, *params, kmem, vmem)


def _causal_keep_t(tk, tq, key_offset):
    r = lax.broadcasted_iota(jnp.int32, (tk, tq), 0)
    c = lax.broadcasted_iota(jnp.int32, (tk, tq), 1)
    return (r + key_offset) <= c


def _init_limits(lim_ref, tk):
    for c in range(lim_ref.shape[1]):
        lim_ref[0, c] = jnp.full((tk, MXU_DIM), SCORE_CEIL, F32)
        lim_ref[1, c] = jnp.where(_causal_keep_t(tk, MXU_DIM, -c * MXU_DIM), SCORE_CEIL, NEG_INF)


def _score_strip(k_blk, q_strip, lim_ref, diag, slot, idx, c):
    s_ref, mb_ref = slot
    s = lax.dot_general(k_blk, q_strip, _NT, preferred_element_type=F32)
    if diag is not None:
        s = jnp.minimum(s, lim_ref[diag, c])
    s_ref[idx, c] = s
    mb_ref[idx, c] = jnp.max(s, axis=0, keepdims=True)


def _softmax_pv_strip(slot, idx, c, vt_blk, first, m_ref, acc_ref, qi):
    s_ref, mb_ref = slot
    s = s_ref[idx, c]
    m_old = jnp.where(first, NEG_INF, m_ref[qi, idx, c])
    m_new = jnp.maximum(m_old, mb_ref[idx, c])
    alpha = jnp.exp2(m_old - m_new)
    p = jnp.exp2(s - m_new).astype(BF16)
    acc_ref[qi, idx, c] = acc_ref[qi, idx, c] * alpha + jnp.dot(vt_blk, p, preferred_element_type=F32)
    m_ref[qi, idx, c] = m_new


def _causal_steps(nq):
    below = [(i, j) for i in range(nq) for j in range(i)]
    qi = np.array([i for i in range(nq)] + [i for i, _ in below], np.int32)
    kj = np.array([i for i in range(nq)] + [j for _, j in below], np.int32)
    first = (np.arange(len(qi)) < nq).astype(np.int32)
    trips = (len(qi) - 1) // TRIP_STEPS
    on_diag = (qi == kj)[1:1 + trips * TRIP_STEPS].reshape(trips, TRIP_STEPS)
    return qi, kj, first, np.append(on_diag.any(axis=1), False).astype(np.int32)


def _flat_sweep(n_steps, n_diag, trip_diag, scores, update):
    def pair(t0, u, with_scores, masked):
        ahead = scores(t0 + u + 1, (u + 1) % SCORE_SLOTS, masked) if with_scores else []
        now = update(t0 + u, u % SCORE_SLOTS)
        for i in range(max(len(ahead), len(now))):
            if i < len(ahead):
                ahead[i]()
            if i < len(now):
                now[i]()

    for piece in scores(0, 0, True):
        piece()
    trips = (n_steps - 1) // TRIP_STEPS

    def trip(i, masked):
        for u in range(TRIP_STEPS):
            pair(i * TRIP_STEPS, u, True, masked)

    def body(i, carry):
        lax.cond(trip_diag[i] != 0, functools.partial(trip, i, True), functools.partial(trip, i, False))
        return carry

    lax.fori_loop(0, trips, body, 0)
    done = trips * TRIP_STEPS
    for u in range(n_steps - done):
        pair(done, u, done + u + 1 < n_steps, done + u + 1 < n_diag)


def _normalised_t(acc_ref, qb, idx):
    strips = []
    for c in range(acc_ref.shape[2]):
        acc = acc_ref[qb, idx, c]
        strips.append(acc[:V_DIM] * (1.0 / acc[V_DIM:V_DIM + 1]))
    return jnp.concatenate(strips, axis=1)


def _attn_scratch(count, nq, tk, tq):
    ns = tq // MXU_DIM
    shapes = []
    for _ in range(SCORE_SLOTS):
        shapes += [pltpu.VMEM((count, ns, tk, MXU_DIM), F32), pltpu.VMEM((count, ns, 1, MXU_DIM), F32)]
    return shapes + [pltpu.VMEM((2, ns, tk, MXU_DIM), F32), pltpu.VMEM((nq, count, ns, 1, MXU_DIM), F32),
                     pltpu.VMEM((nq, count, ns, VT_ROWS, MXU_DIM), F32)]


def _split_scratch(scratch, tk):
    slots = [(scratch[2 * i], scratch[2 * i + 1]) for i in range(SCORE_SLOTS)]
    lim_ref, m_ref, acc_ref = scratch[2 * SCORE_SLOTS:]
    _init_limits(lim_ref, tk)
    m_ref[...] = jnp.full(m_ref.shape, NEG_INF, F32)
    acc_ref[...] = jnp.zeros(acc_ref.shape, F32)
    return slots, lim_ref, m_ref, acc_ref


def _mla_attn_kernel(qi_tab, kj_tab, first_tab, trip_diag, q0_ref, q1_ref, k0_ref, k1_ref, vt_ref, gate_ref, o_ref,
                     *scratch, tq, tk, nq):
    slots, lim_ref, m_ref, acc_ref = _split_scratch(scratch, tk)
    q_refs, k_refs = (q0_ref, q1_ref), (k0_ref, k1_ref)
    n_strips = tq // MXU_DIM

    def scores(t, slot, masked):
        qi, kj = qi_tab[t], kj_tab[t]
        krows = pl.ds(pl.multiple_of(kj * tk, tk), tk)
        diag = (kj == qi).astype(jnp.int32) if masked else None

        def piece(hd, c):
            qrows = pl.ds(pl.multiple_of(qi * tq + c * MXU_DIM, MXU_DIM), MXU_DIM)
            _score_strip(k_refs[hd][krows, :], q_refs[hd][qrows, :], lim_ref, diag, slots[slot], hd, c)

        return [functools.partial(piece, hd, c) for hd in range(2) for c in range(n_strips)]

    def update(t, slot):
        qi, kj = qi_tab[t], kj_tab[t]

        def piece(hd, c):
            _softmax_pv_strip(slots[slot], hd, c, vt_ref[kj, hd * VT_ROWS:(hd + 1) * VT_ROWS, :],
                              first_tab[t] != 0, m_ref, acc_ref, qi)

        return [functools.partial(piece, hd, c) for hd in range(2) for c in range(n_strips)]

    _flat_sweep(nq * (nq + 1) // 2, nq, trip_diag, scores, update)

    def finish(qb, carry):
        rows = pl.ds(pl.multiple_of(qb * tq, tq), tq)
        o_t = jnp.concatenate([_normalised_t(acc_ref, qb, 0), _normalised_t(acc_ref, qb, 1)], axis=0)
        o_ref[rows, :] = (o_t.T * gate_ref[rows, :].astype(F32)).astype(BF16)
        return carry

    lax.fori_loop(0, nq, finish, 0)


def _mla_attn_call(q, k, vt, gate, batch, seq):
    tq, tk = TQ_MLA, TK
    nq = seq // tq
    pairs = MLA_HEADS // 2
    step_tabs = _causal_steps(nq)
    blk = lambda b, hp, *_: (b, hp)
    head0 = pl.BlockSpec((seq, HEAD_PAD), lambda b, hp, *_: (b, 2 * hp))
    head1 = pl.BlockSpec((seq, HEAD_PAD), lambda b, hp, *_: (b, 2 * hp + 1))
    grid_spec = pltpu.PrefetchScalarGridSpec(
        num_scalar_prefetch=len(step_tabs),
        grid=(batch, pairs),
        in_specs=[head0, head1, head0, head1,
                  pl.BlockSpec((seq // tk, 2 * VT_ROWS, tk), lambda b, hp, *_: (b, hp, 0)),
                  pl.BlockSpec((seq, LANES), blk)],
        out_specs=pl.BlockSpec((seq, LANES), blk),
        scratch_shapes=_attn_scratch(2, nq, tk, tq))
    return pl.pallas_call(
        functools.partial(_mla_attn_kernel, tq=tq, tk=tk, nq=nq),
        grid_spec=grid_spec,
        out_shape=jax.ShapeDtypeStruct((batch * seq, MLA_W), BF16),
        compiler_params=pltpu.CompilerParams(dimension_semantics=("arbitrary",) * 2, vmem_limit_bytes=VMEM_LIMIT),
        name="mla_attn",
    )(*step_tabs, q, q, k, k, vt, gate)


def _diff_attn_kernel(qi_tab, kj_tab, first_tab, trip_diag, q_ref, k_ref, vt_ref, gate_ref, lam_ref, sg_ref, o_ref,
                      *scratch, tq, tk, nq, lam_init):
    slots, lim_ref, m_ref, acc_ref = _split_scratch(scratch, tk)
    n_strips = tq // MXU_DIM
    map_of_lane = _lane_iota((1, LANES)) // DIFF_D

    def scores(t, slot, masked):
        qi, kj = qi_tab[t], kj_tab[t]
        krows = pl.ds(pl.multiple_of(kj * tk, tk), tk)
        diag = (kj == qi).astype(jnp.int32) if masked else None

        def piece(idx, c):
            q_strip = q_ref[pl.ds(pl.multiple_of(qi * tq + c * MXU_DIM, MXU_DIM), MXU_DIM), :]
            q_map = jnp.where(map_of_lane == idx, q_strip, jnp.zeros_like(q_strip))
            _score_strip(k_ref[krows, :], q_map, lim_ref, diag, slots[slot], idx, c)

        return [functools.partial(piece, idx, c) for idx in range(4) for c in range(n_strips)]

    def update(t, slot):
        qi, kj = qi_tab[t], kj_tab[t]

        def piece(idx, c):
            hd = idx // 2
            _softmax_pv_strip(slots[slot], idx, c, vt_ref[kj, hd * VT_ROWS:(hd + 1) * VT_ROWS, :],
                              first_tab[t] != 0, m_ref, acc_ref, qi)

        return [functools.partial(piece, idx, c) for idx in range(4) for c in range(n_strips)]

    _flat_sweep(nq * (nq + 1) // 2, nq, trip_diag, scores, update)

    lv = lam_ref[...]
    lam = (jnp.exp(jnp.sum(lv[0:1] * lv[1:2], axis=-1, keepdims=True))
           - jnp.exp(jnp.sum(lv[2:3] * lv[3:4], axis=-1, keepdims=True)) + lam_init)

    def finish(qb, carry):
        rows = pl.ds(pl.multiple_of(qb * tq, tq), tq)
        heads_t = []
        for hd in range(2):
            o = _normalised_t(acc_ref, qb, 2 * hd) - lam * _normalised_t(acc_ref, qb, 2 * hd + 1)
            heads_t.append(o * lax.rsqrt(jnp.mean(o * o, axis=0, keepdims=True) + RMS_EPS))
        o = jnp.concatenate(heads_t, axis=0).T * sg_ref[...] * (1.0 - lam_init)
        o_ref[rows, :] = (o * gate_ref[rows, :].astype(F32)).astype(BF16)
        return carry

    lax.fori_loop(0, nq, finish, 0)


def _diff_attn_call(q, k, vt, gate, lam_pad, subln_t, layer, batch, seq, lam_init):
    tq, tk = TQ_DIFF, TK
    nq = seq // tq
    pairs = DIFF_HEADS // 2
    gate_off = MLA_W // LANES
    step_tabs = _causal_steps(nq)
    blk = lambda b, hp, *_: (b, hp)
    grid_spec = pltpu.PrefetchScalarGridSpec(
        num_scalar_prefetch=len(step_tabs),
        grid=(batch, pairs),
        in_specs=[pl.BlockSpec((seq, LANES), blk),
                  pl.BlockSpec((seq, LANES), blk),
                  pl.BlockSpec((seq // tk, 2 * VT_ROWS, tk), lambda b, hp, *_: (b, hp, 0)),
                  pl.BlockSpec((seq, LANES), lambda b, hp, *_: (b, gate_off + hp)),
                  _layer_spec(lam_pad, layer),
                  _layer_spec(subln_t, layer)],
        out_specs=pl.BlockSpec((seq, LANES), blk),
        scratch_shapes=_attn_scratch(4, nq, tk, tq))
    return pl.pallas_call(
        functools.partial(_diff_attn_kernel, tq=tq, tk=tk, nq=nq, lam_init=lam_init),
        grid_spec=grid_spec,
        out_shape=jax.ShapeDtypeStruct((batch * seq, DIFF_W), BF16),
        compiler_params=pltpu.CompilerParams(dimension_semantics=("arbitrary",) * 2, vmem_limit_bytes=VMEM_LIMIT),
        name="diff_attn",
    )(*step_tabs, q, k, vt, gate, lam_pad, subln_t)


def _out_kernel(x_ref, ymla_ref, ydiff_ref, ymem_ref, w_ref, o_ref):
    y = jnp.concatenate([ymla_ref[...], ydiff_ref[...], ymem_ref[...]], axis=1)
    o_ref[...] = x_ref[...] + jnp.dot(y, w_ref[...], preferred_element_type=F32)


def _out_call(x2d, y_mla, y_diff, y_mem, w_out, layer):
    n = x2d.shape[0]
    tm = TM_OUT
    row = lambda i: (i, 0)
    return pl.pallas_call(
        _out_kernel,
        grid=(n // tm,),
        in_specs=[pl.BlockSpec((tm, D_MODEL), row), pl.BlockSpec((tm, MLA_W), row),
                  pl.BlockSpec((tm, DIFF_W), row), pl.BlockSpec((tm, MEM_W), row),
                  _layer_spec(w_out, layer)],
        out_specs=pl.BlockSpec((tm, D_MODEL), row),
        out_shape=jax.ShapeDtypeStruct((n, D_MODEL), F32),
        compiler_params=pltpu.CompilerParams(dimension_semantics=("arbitrary",), vmem_limit_bytes=VMEM_LIMIT),
        name="out_proj",
    )(x2d, y_mla, y_diff, y_mem, w_out)


def _block_diag_ones(seg):
    idx = np.arange(MXU_DIM) // seg
    return jnp.asarray(idx[:, None] == idx[None, :], dtype=BF16)


def _pair_swapped(a):
    lead = a.shape[:-1]
    a4 = a.reshape(lead + (a.shape[-1] // ROPE_D, 2, ROPE_D // 2))
    return a4[..., ::-1, :].reshape(a.shape)


def _head_padded(a, heads, width, offset=0):
    lead = a.shape[:-1]
    a3 = a.reshape(lead + (heads, width))
    pad = [(0, 0)] * (a3.ndim - 1) + [(offset, LANES - width - offset)]
    return jnp.pad(a3, pad).reshape(lead + (heads * LANES,))


def _layer_spec(a, layer):
    zeros = (0,) * (a.ndim - 1)
    return pl.BlockSpec((None,) + a.shape[1:], lambda *_: (layer,) + zeros)


def _stacked_params(norm_g, w_in, mla_q_norm_g, mla_kv_norm_g, w_uq, w_ukv, mla_qn_g, mla_kn_g,
                    diff_qn_g, diff_kn_g, mem_qn_g):
    depth = w_in.shape[0]
    o_cq, o_ckv, o_kr = 0, Q_LORA, Q_LORA + KV_LORA
    o_dq = o_kr + MLA_ROPE
    o_dk = o_dq + DIFF_QK_W
    o_dv = o_dk + DIFF_QK_W
    o_mq = o_dv + DIFF_W
    o_z = o_mq + MEM_W
    w = w_in
    w_kr, w_dq, w_dk = w[..., o_kr:o_dq], w[..., o_dq:o_dk], w[..., o_dk:o_dv]
    w_in_p = jnp.concatenate([
        w[..., o_cq:o_ckv], w[..., o_ckv:o_kr],
        _head_padded(w_kr, 1, MLA_ROPE, MLA_NOPE), _head_padded(_pair_swapped(w_kr), 1, MLA_ROPE, MLA_NOPE),
        w_dq, _pair_swapped(w_dq), w_dk, _pair_swapped(w_dk),
        _head_padded(w[..., o_dv:o_mq], DIFF_HEADS, DIFF_V),
        w[..., o_mq:o_z], w[..., o_z:]], axis=-1).astype(BF16)

    uq4 = w_uq.reshape(depth, Q_LORA, MLA_HEADS, MLA_QK)
    w_uq_p = _head_padded(w_uq, MLA_HEADS, MLA_QK).astype(BF16)
    w_uq_sw = _head_padded(_pair_swapped(uq4[..., MLA_NOPE:]).reshape(depth, Q_LORA, MLA_HEADS * MLA_ROPE),
                           MLA_HEADS, MLA_ROPE, MLA_NOPE).astype(BF16)
    kv4 = w_ukv.reshape(depth, KV_LORA, MLA_HEADS, MLA_NOPE + MLA_V)
    w_uk_p = _head_padded(kv4[..., :MLA_NOPE].reshape(depth, KV_LORA, MLA_HEADS * MLA_NOPE),
                          MLA_HEADS, MLA_NOPE).astype(BF16)
    w_uv_t = jnp.pad(jnp.transpose(kv4[..., MLA_NOPE:], (0, 2, 3, 1)),
                     ((0, 0), (0, 0), (0, VT_ROWS - MLA_V), (0, 0))
                     ).reshape(depth, MLA_HEADS * VT_ROWS, KV_LORA).astype(BF16)

    q_scale = LOG2_E / math.sqrt(MLA_QK)
    d_scale = LOG2_E / math.sqrt(DIFF_D)
    row = lambda v: v[:, None, :]
    rope_vec = lambda g: row(jnp.pad(g[:, MLA_NOPE:], ((0, 0), (MLA_NOPE, LANES - MLA_QK))))
    rope_vec_sw = lambda g: row(jnp.pad(_pair_swapped(g[:, MLA_NOPE:]), ((0, 0), (MLA_NOPE, LANES - MLA_QK))))
    nope_vec = lambda g: row(jnp.pad(g[:, :MLA_NOPE], ((0, 0), (0, LANES - MLA_NOPE))))
    dqn = row(jnp.tile(diff_qn_g, (1, 2 * DIFF_HEADS)))
    dkn = row(jnp.tile(diff_kn_g, (1, 2 * DIFF_HEADS)))
    return {
        "gx": row(norm_g),
        "w_in": w_in_p,
        "gcq": row(mla_q_norm_g),
        "gckv": row(mla_kv_norm_g),
        "w_uq": w_uq_p, "w_uq_sw": w_uq_sw, "w_uk": w_uk_p, "w_uv_t": w_uv_t,
        "q_gc1": rope_vec(mla_qn_g) * q_scale, "q_gc0": nope_vec(mla_qn_g) * q_scale,
        "q_gs": rope_vec_sw(mla_qn_g) * q_scale,
        "k_gc1": rope_vec(mla_kn_g), "k_gc0": nope_vec(mla_kn_g), "k_gs": rope_vec_sw(mla_kn_g),
        "dq_gc": dqn * d_scale, "dq_gs": _pair_swapped(dqn) * d_scale,
        "dk_gc": dkn, "dk_gs": _pair_swapped(dkn),
        "mqn": row(jnp.tile(mem_qn_g, (1, MEM_HEADS))) * (1.0 / math.sqrt(MEM_D)),
        "bd_diff": jnp.broadcast_to(_block_diag_ones(DIFF_D), (depth, MXU_DIM, MXU_DIM)),
        "bd_mem": jnp.broadcast_to(_block_diag_ones(MEM_D), (depth, MXU_DIM, MXU_DIM)),
    }


def kernel(x, mem, positions, norm_g, w_in, mla_q_norm_g, mla_kv_norm_g, w_uq, w_ukv, mla_qn_g, mla_kn_g,
           diff_qn_g, diff_kn_g, diff_lambda, diff_subln_g, mem_norm_g, w_mem_kv, mem_qn_g, mem_kn_g, w_out):
    batch, seq, d_model = x.shape
    depth = w_in.shape[0]
    assert d_model == D_MODEL and MLA_V == DIFF_V and TQ_MLA == TK and TQ_DIFF == TK
    assert TRIP_STEPS % SCORE_SLOTS == 0 and SCORE_SLOTS >= 2 and TK % MXU_DIM == 0
    assert seq % max(TM_PROJ, TQ_MLA, TQ_DIFF, TM_OUT) == 0 and (batch * seq) % 1024 == 0
    cos_t, sin_t = _rope_tables(positions)
    kmem_all, vmem_all = _mem_kv(mem, mem_norm_g, w_mem_kv, mem_kn_g, _block_diag_ones(MEM_D))
    p = _stacked_params(norm_g, w_in, mla_q_norm_g, mla_kv_norm_g, w_uq, w_ukv, mla_qn_g, mla_kn_g,
                        diff_qn_g, diff_kn_g, mem_qn_g)
    lam_pad = jnp.pad(diff_lambda, ((0, 0), (0, 4), (0, LANES - DIFF_D)))
    subln_t = jnp.tile(diff_subln_g, (1, 2))[:, None, :]
    w_out_b = w_out.astype(BF16)
    x2d = x.reshape(batch * seq, D_MODEL)
    for l in range(depth):
        lam_init = 0.8 - 0.6 * math.exp(-0.3 * l)
        q, k, v, dq, dk, dv, y_mem, gate = _proj_call(x2d, cos_t, sin_t, kmem_all, vmem_all, p, l, seq)
        y_mla = _mla_attn_call(q, k, v, gate, batch, seq)
        y_diff = _diff_attn_call(dq, dk, dv, gate, lam_pad, subln_t, l, batch, seq, lam_init)
        x2d = _out_call(x2d, y_mla, y_diff, y_mem, w_out_b, l)
    return x2d.reshape(batch, seq, D_MODEL)
```

```python
import functools
import math

import numpy as np
import jax
import jax.numpy as jnp
from jax import lax
from jax.experimental import pallas as pl
from jax.experimental.pallas import tpu as pltpu

F32 = jnp.float32
BF16 = jnp.bfloat16

D_MODEL = 1024
ROPE_THETA = 10000.0
RMS_EPS = 1e-6
NEG_INF = -1e30
LOG2_E = math.log2(math.e)

MLA_HEADS = 8
Q_LORA = 256
KV_LORA = 128
MLA_NOPE = 64
MLA_ROPE = 32
MLA_QK = MLA_NOPE + MLA_ROPE
MLA_V = 64
MLA_W = MLA_HEADS * MLA_V

DIFF_HEADS = 4
DIFF_D = 32
DIFF_V = 2 * DIFF_D
DIFF_W = DIFF_HEADS * DIFF_V
DIFF_QK_W = 2 * DIFF_HEADS * DIFF_D

MEM_HEADS = 4
MEM_D = 64
MEM_W = MEM_HEADS * MEM_D

D_MIX = MLA_W + DIFF_W + MEM_W

LANES = 128
MXU_DIM = 256
ROPE_D = 32
HEAD_PAD = LANES

_C_CQ = 0
_C_CKV = _C_CQ + Q_LORA
_C_KR = _C_CKV + KV_LORA
_C_KR_SW = _C_KR + LANES
_C_DQ = _C_KR_SW + LANES
_C_DQ_SW = _C_DQ + DIFF_QK_W
_C_DK = _C_DQ_SW + DIFF_QK_W
_C_DK_SW = _C_DK + DIFF_QK_W
_C_DV = _C_DK_SW + DIFF_QK_W
_C_MQ = _C_DV + DIFF_HEADS * LANES
_C_Z = _C_MQ + MEM_W
_C_END = _C_Z + D_MIX

VMEM_LIMIT = 56 * 1024 * 1024

V_DIM = MLA_V
VT_ROWS = 80

TK = 512
TM_PROJ = TK
TM_OUT = 512
TQ_MLA = 512
TQ_DIFF = 512
SCORE_SLOTS = 2
TRIP_STEPS = 8
SCORE_CEIL = 3.0e38

_NT = (((1,), (1,)), ((), ()))


def _rms(x):
    return x * lax.rsqrt(jnp.mean(x * x, axis=-1, keepdims=True) + RMS_EPS)


def _lane_iota(shape):
    return lax.broadcasted_iota(jnp.int32, shape, len(shape) - 1)


def _seg_sum_sq(x, bd_ref):
    sq = x * x
    hi = sq.astype(BF16)
    lo = (sq - hi.astype(F32)).astype(BF16)
    bd = bd_ref[...]
    return jnp.dot(hi, bd, preferred_element_type=F32) + jnp.dot(lo, bd, preferred_element_type=F32)


def _rope_table_kernel(pos_ref, inv_ref, cos_ref, sin_ref):
    ang = pos_ref[...].astype(F32) * inv_ref[...]
    lane = _lane_iota(ang.shape)
    sign = jnp.where((lane & (ROPE_D // 2)) == 0, -1.0, 1.0)
    cos_ref[...] = jnp.cos(ang)
    sin_ref[...] = jnp.sin(ang) * sign


def _rope_tables(positions):
    n = positions.size
    tm = 1024
    inv = ROPE_THETA ** (-jnp.arange(0, ROPE_D, 2, dtype=F32) / ROPE_D)
    inv_t = jnp.tile(inv, LANES // (ROPE_D // 2))[None, :]
    pos = positions.reshape(n, 1)
    return pl.pallas_call(
        _rope_table_kernel,
        grid=(n // tm,),
        in_specs=[pl.BlockSpec((tm, 1), lambda i: (i, 0)),
                  pl.BlockSpec((1, LANES), lambda i: (0, 0))],
        out_specs=[pl.BlockSpec((tm, LANES), lambda i: (i, 0)),
                   pl.BlockSpec((tm, LANES), lambda i: (i, 0))],
        out_shape=[jax.ShapeDtypeStruct((n, LANES), F32)] * 2,
        name="rope_tables",
    )(pos, inv_t)


def _mem_kv_kernel(mem_ref, g_ref, w_ref, kn_ref, bd_ref, k_ref, v_ref):
    m = mem_ref[0]
    mn = (_rms(m) * g_ref[0]).astype(BF16)
    kv = jnp.dot(mn, w_ref[0], preferred_element_type=F32)
    k = kv[:, :MEM_W]
    k = k * lax.rsqrt(_seg_sum_sq(k, bd_ref) * (1.0 / MEM_D) + RMS_EPS) * kn_ref[0]
    k_ref[0, 0] = k.astype(BF16)
    v_ref[0, 0] = kv[:, MEM_W:].astype(BF16)


def _mem_kv(mem, mem_norm_g, w_mem_kv, mem_kn_g, bd_mem):
    depth = w_mem_kv.shape[0]
    b, m, d = mem.shape
    kn_t = jnp.tile(mem_kn_g, (1, MEM_HEADS))[:, None, :]
    return pl.pallas_call(
        _mem_kv_kernel,
        grid=(depth, b),
        in_specs=[pl.BlockSpec((1, m, d), lambda l, i: (i, 0, 0)),
                  pl.BlockSpec((1, 1, d), lambda l, i: (l, 0, 0)),
                  pl.BlockSpec((1, d, 2 * MEM_W), lambda l, i: (l, 0, 0)),
                  pl.BlockSpec((1, 1, MEM_W), lambda l, i: (l, 0, 0)),
                  pl.BlockSpec(bd_mem.shape, lambda l, i: (0, 0))],
        out_specs=[pl.BlockSpec((1, 1, m, MEM_W), lambda l, i: (l, i, 0, 0)),
                   pl.BlockSpec((1, 1, m, MEM_W), lambda l, i: (l, i, 0, 0))],
        out_shape=[jax.ShapeDtypeStruct((depth, b, m, MEM_W), BF16)] * 2,
        name="mem_kv",
    )(mem, mem_norm_g[:, None, :], w_mem_kv.astype(BF16), kn_t, bd_mem)


_PROJ_PARAMS = ("gx", "w_in", "gcq", "gckv", "w_uq", "w_uq_sw", "w_uk", "w_uv_t",
                "q_gc1", "q_gc0", "q_gs", "k_gc1", "k_gc0", "k_gs",
                "dq_gc", "dq_gs", "dk_gc", "dk_gs", "mqn", "bd_diff", "bd_mem")


def _proj_kernel(x_ref, cos_ref, sin_ref, gx_ref, win_ref, gcq_ref, gckv_ref, wuq_ref, wuqsw_ref, wuk_ref, wuvt_ref,
                 qgc1_ref, qgc0_ref, qgs_ref, kgc1_ref, kgc0_ref, kgs_ref,
                 dqgc_ref, dqgs_ref, dkgc_ref, dkgs_ref, mqn_ref, bdd_ref, bdm_ref, kmem_ref, vmem_ref,
                 q_out, k_out, v_out, dq_out, dk_out, dv_out, ymem_out, gate_out):
    tm = x_ref.shape[0]
    h = (_rms(x_ref[...]) * gx_ref[...]).astype(BF16)
    proj = jnp.dot(h, win_ref[...], preferred_element_type=F32)
    cos = cos_ref[...]
    sin = sin_ref[...]

    cqn = (_rms(proj[:, _C_CQ:_C_CQ + Q_LORA]) * gcq_ref[...]).astype(BF16)
    q = jnp.dot(cqn, wuq_ref[...], preferred_element_type=F32)
    q_sw = jnp.dot(cqn, wuqsw_ref[...], preferred_element_type=F32)
    q_gc = cos * qgc1_ref[...] + qgc0_ref[...]
    q_gs = sin * qgs_ref[...]
    for hd in range(MLA_HEADS):
        sl = slice(hd * HEAD_PAD, (hd + 1) * HEAD_PAD)
        qh = q[:, sl]
        r = lax.rsqrt(jnp.sum(qh * qh, axis=-1, keepdims=True) * (1.0 / MLA_QK) + RMS_EPS)
        q_out[:, sl] = ((qh * q_gc + q_sw[:, sl] * q_gs) * r).astype(BF16)

    ckv = _rms(proj[:, _C_CKV:_C_CKV + KV_LORA]) * gckv_ref[...]
    ckvn = ckv.astype(BF16)
    k_nope = jnp.dot(ckvn, wuk_ref[...], preferred_element_type=F32)
    kr = proj[:, _C_KR:_C_KR + LANES]
    k_gc = cos * kgc1_ref[...] + kgc0_ref[...]
    kr_rot = proj[:, _C_KR_SW:_C_KR_SW + LANES] * (sin * kgs_ref[...])
    for hd in range(MLA_HEADS):
        sl = slice(hd * HEAD_PAD, (hd + 1) * HEAD_PAD)
        kh = k_nope[:, sl] + kr
        r = lax.rsqrt(jnp.sum(kh * kh, axis=-1, keepdims=True) * (1.0 / MLA_QK) + RMS_EPS)
        k_out[:, sl] = ((kh * k_gc + kr_rot) * r).astype(BF16)
    vt = jnp.dot(wuvt_ref[...], ckv.T.astype(BF16), preferred_element_type=F32)
    ones_row = (lax.broadcasted_iota(jnp.int32, (VT_ROWS, tm), 0) == V_DIM).astype(F32)
    for hd in range(MLA_HEADS):
        rows = slice(hd * VT_ROWS, (hd + 1) * VT_ROWS)
        v_out[0, rows, :] = (vt[rows] + ones_row).astype(BF16)

    for raw_c, sw_c, gc_ref, gs_ref, out in ((_C_DQ, _C_DQ_SW, dqgc_ref, dqgs_ref, dq_out),
                                             (_C_DK, _C_DK_SW, dkgc_ref, dkgs_ref, dk_out)):
        raw = proj[:, raw_c:raw_c + DIFF_QK_W]
        r = lax.rsqrt(_seg_sum_sq(raw, bdd_ref) * (1.0 / DIFF_D) + RMS_EPS)
        for c in range(DIFF_QK_W // LANES):
            sl = slice(c * LANES, (c + 1) * LANES)
            rot = raw[:, sl] * (cos * gc_ref[:, sl]) + proj[:, sw_c + c * LANES:sw_c + (c + 1) * LANES] * (sin * gs_ref[:, sl])
            out[:, sl] = (rot * r[:, sl]).astype(BF16)
    ones_lane = (_lane_iota((1, LANES)) == V_DIM).astype(F32)
    for hd in range(DIFF_HEADS):
        dv_t = (proj[:, _C_DV + hd * LANES:_C_DV + (hd + 1) * LANES] + ones_lane).T
        dv_out[0, hd * VT_ROWS:(hd + 1) * VT_ROWS, :] = dv_t[:VT_ROWS].astype(BF16)

    z = proj[:, _C_Z:_C_END]
    gate = z * (1.0 / (1.0 + jnp.exp(-z)))
    gate_out[...] = gate[:, :MLA_W + DIFF_W].astype(BF16)

    mq = proj[:, _C_MQ:_C_MQ + MEM_W]
    mq = mq * lax.rsqrt(_seg_sum_sq(mq, bdm_ref) * (1.0 / MEM_D) + RMS_EPS) * mqn_ref[...]
    kmem = kmem_ref[0]
    vmem = vmem_ref[0]
    lane_w = _lane_iota((1, MEM_W))
    o = jnp.zeros(mq.shape, F32)
    for hd in range(MEM_HEADS):
        head_mask = (lane_w // MEM_D) == hd
        qh = jnp.where(head_mask, mq, 0.0).astype(BF16)
        s = lax.dot_general(qh, kmem, _NT, preferred_element_type=F32)
        p = jnp.exp(s - jnp.max(s, axis=-1, keepdims=True))
        inv_l = 1.0 / jnp.sum(p, axis=-1, keepdims=True)
        vh = jnp.where(head_mask, vmem, jnp.zeros_like(vmem))
        o = o + jnp.dot(p.astype(BF16), vh, preferred_element_type=F32) * inv_l
    ymem_out[...] = (o * gate[:, MLA_W + DIFF_W:]).astype(BF16)


def _proj_call(x2d, cos_t, sin_t, kmem, vmem, p, layer, seq):
    n = x2d.shape[0]
    tm = TM_PROJ
    per_batch = seq // tm
    row = lambda i: (i, 0)
    params = [p[name] for name in _PROJ_PARAMS]
    mem_spec = pl.BlockSpec((None, 1) + kmem.shape[2:], lambda i: (layer, i // per_batch, 0, 0))
    in_specs = ([pl.BlockSpec((tm, D_MODEL), row), pl.BlockSpec((tm, LANES), row), pl.BlockSpec((tm, LANES), row)]
                + [_layer_spec(a, layer) for a in params] + [mem_spec, mem_spec])

    def rows_out(width):
        return pl.BlockSpec((tm, width), row), jax.ShapeDtypeStruct((n, width), BF16)

    def values_t_out(heads):
        return (pl.BlockSpec((1, heads * VT_ROWS, tm), lambda i: (i, 0, 0)),
                jax.ShapeDtypeStruct((n // tm, heads * VT_ROWS, tm), BF16))

    outs = [rows_out(MLA_HEADS * HEAD_PAD), rows_out(MLA_HEADS * HEAD_PAD), values_t_out(MLA_HEADS),
            rows_out(DIFF_QK_W), rows_out(DIFF_QK_W), values_t_out(DIFF_HEADS),
            rows_out(MEM_W), rows_out(MLA_W + DIFF_W)]
    return pl.pallas_call(
        _proj_kernel,
        grid=(n // tm,),
        in_specs=in_specs,
        out_specs=[o[0] for o in outs],
        out_shape=[o[1] for o in outs],
        compiler_params=pltpu.CompilerParams(dimension_semantics=("arbitrary",), vmem_limit_bytes=VMEM_LIMIT),
        name="proj_prep",
    )(x2d, cos_t, sin_t, *params, kmem, vmem)


def _causal_keep_t(tk, tq, key_offset):
    r = lax.broadcasted_iota(jnp.int32, (tk, tq), 0)
    c = lax.broadcasted_iota(jnp.int32, (tk, tq), 1)
    return (r + key_offset) <= c


def _init_limits(lim_ref, tk):
    for c in range(lim_ref.shape[1]):
        lim_ref[0, c] = jnp.full((tk, MXU_DIM), SCORE_CEIL, F32)
        lim_ref[1, c] = jnp.where(_causal_keep_t(tk, MXU_DIM, -c * MXU_DIM), SCORE_CEIL, NEG_INF)


def _score_strip(k_blk, q_strip, lim_ref, diag, slot, idx, c):
    s_ref, mb_ref = slot
    s = lax.dot_general(k_blk, q_strip, _NT, preferred_element_type=F32)
    if diag is not None:
        s = jnp.minimum(s, lim_ref[diag, c])
    s_ref[idx, c] = s
    mb_ref[idx, c] = jnp.max(s, axis=0, keepdims=True)


def _softmax_pv_strip(slot, idx, c, vt_blk, first, m_ref, acc_ref, qi):
    s_ref, mb_ref = slot
    s = s_ref[idx, c]
    m_old = jnp.where(first, NEG_INF, m_ref[qi, idx, c])
    m_new = jnp.maximum(m_old, mb_ref[idx, c])
    alpha = jnp.exp2(m_old - m_new)
    acc = acc_ref[qi, idx, c] * alpha
    for k0 in range(0, s.shape[0], MXU_DIM):
        p = jnp.exp2(s[k0:k0 + MXU_DIM] - m_new).astype(BF16)
        acc = acc + jnp.dot(vt_blk[:, k0:k0 + MXU_DIM], p, preferred_element_type=F32)
    acc_ref[qi, idx, c] = acc
    m_ref[qi, idx, c] = m_new


def _causal_steps(nq):
    below = [(i, j) for i in range(nq) for j in range(i)]
    qi = np.array([i for i in range(nq)] + [i for i, _ in below], np.int32)
    kj = np.array([i for i in range(nq)] + [j for _, j in below], np.int32)
    first = (np.arange(len(qi)) < nq).astype(np.int32)
    trips = (len(qi) - 1) // TRIP_STEPS
    on_diag = (qi == kj)[1:1 + trips * TRIP_STEPS].reshape(trips, TRIP_STEPS)
    return qi, kj, first, np.append(on_diag.any(axis=1), False).astype(np.int32)


def _flat_sweep(n_steps, n_diag, trip_diag, scores, update):
    def pair(t0, u, with_scores, masked):
        ahead = scores(t0 + u + 1, (u + 1) % SCORE_SLOTS, masked) if with_scores else []
        now = update(t0 + u, u % SCORE_SLOTS)
        for i in range(max(len(ahead), len(now))):
            if i < len(ahead):
                ahead[i]()
            if i < len(now):
                now[i]()

    for piece in scores(0, 0, True):
        piece()
    trips = (n_steps - 1) // TRIP_STEPS

    def trip(i, masked):
        for u in range(TRIP_STEPS):
            pair(i * TRIP_STEPS, u, True, masked)

    def body(i, carry):
        lax.cond(trip_diag[i] != 0, functools.partial(trip, i, True), functools.partial(trip, i, False))
        return carry

    lax.fori_loop(0, trips, body, 0)
    done = trips * TRIP_STEPS
    for u in range(n_steps - done):
        pair(done, u, done + u + 1 < n_steps, done + u + 1 < n_diag)


def _normalised_t(acc_ref, qb, idx):
    strips = []
    for c in range(acc_ref.shape[2]):
        acc = acc_ref[qb, idx, c]
        strips.append(acc[:V_DIM] * (1.0 / acc[V_DIM:V_DIM + 1]))
    return jnp.concatenate(strips, axis=1)


def _attn_scratch(count, nq, tk, tq):
    ns = tq // MXU_DIM
    shapes = []
    for _ in range(SCORE_SLOTS):
        shapes += [pltpu.VMEM((count, ns, tk, MXU_DIM), F32), pltpu.VMEM((count, ns, 1, MXU_DIM), F32)]
    return shapes + [pltpu.VMEM((2, ns, tk, MXU_DIM), F32), pltpu.VMEM((nq, count, ns, 1, MXU_DIM), F32),
                     pltpu.VMEM((nq, count, ns, VT_ROWS, MXU_DIM), F32)]


def _split_scratch(scratch, tk):
    slots = [(scratch[2 * i], scratch[2 * i + 1]) for i in range(SCORE_SLOTS)]
    lim_ref, m_ref, acc_ref = scratch[2 * SCORE_SLOTS:]
    _init_limits(lim_ref, tk)
    m_ref[...] = jnp.full(m_ref.shape, NEG_INF, F32)
    acc_ref[...] = jnp.zeros(acc_ref.shape, F32)
    return slots, lim_ref, m_ref, acc_ref


def _mla_attn_kernel(qi_tab, kj_tab, first_tab, trip_diag, q0_ref, q1_ref, k0_ref, k1_ref, vt_ref, gate_ref, o_ref,
                     *scratch, tq, tk, nq):
    slots, lim_ref, m_ref, acc_ref = _split_scratch(scratch, tk)
    q_refs, k_refs = (q0_ref, q1_ref), (k0_ref, k1_ref)
    n_strips = tq // MXU_DIM

    def scores(t, slot, masked):
        qi, kj = qi_tab[t], kj_tab[t]
        krows = pl.ds(pl.multiple_of(kj * tk, tk), tk)
        diag = (kj == qi).astype(jnp.int32) if masked else None

        def piece(hd, c):
            qrows = pl.ds(pl.multiple_of(qi * tq + c * MXU_DIM, MXU_DIM), MXU_DIM)
            _score_strip(k_refs[hd][krows, :], q_refs[hd][qrows, :], lim_ref, diag, slots[slot], hd, c)

        return [functools.partial(piece, hd, c) for hd in range(2) for c in range(n_strips)]

    def update(t, slot):
        qi, kj = qi_tab[t], kj_tab[t]

        def piece(hd, c):
            _softmax_pv_strip(slots[slot], hd, c, vt_ref[kj, hd * VT_ROWS:(hd + 1) * VT_ROWS, :],
                              first_tab[t] != 0, m_ref, acc_ref, qi)

        return [functools.partial(piece, hd, c) for hd in range(2) for c in range(n_strips)]

    _flat_sweep(nq * (nq + 1) // 2, nq, trip_diag, scores, update)

    def finish(qb, carry):
        rows = pl.ds(pl.multiple_of(qb * tq, tq), tq)
        o_t = jnp.concatenate([_normalised_t(acc_ref, qb, 0), _normalised_t(acc_ref, qb, 1)], axis=0)
        o_ref[rows, :] = (o_t.T * gate_ref[rows, :].astype(F32)).astype(BF16)
        return carry

    lax.fori_loop(0, nq, finish, 0)


def _mla_attn_call(q, k, vt, gate, batch, seq):
    tq, tk = TQ_MLA, TK
    nq = seq // tq
    pairs = MLA_HEADS // 2
    step_tabs = _causal_steps(nq)
    blk = lambda b, hp, *_: (b, hp)
    head0 = pl.BlockSpec((seq, HEAD_PAD), lambda b, hp, *_: (b, 2 * hp))
    head1 = pl.BlockSpec((seq, HEAD_PAD), lambda b, hp, *_: (b, 2 * hp + 1))
    grid_spec = pltpu.PrefetchScalarGridSpec(
        num_scalar_prefetch=len(step_tabs),
        grid=(batch, pairs),
        in_specs=[head0, head1, head0, head1,
                  pl.BlockSpec((seq // tk, 2 * VT_ROWS, tk), lambda b, hp, *_: (b, hp, 0)),
                  pl.BlockSpec((seq, LANES), blk)],
        out_specs=pl.BlockSpec((seq, LANES), blk),
        scratch_shapes=_attn_scratch(2, nq, tk, tq))
    return pl.pallas_call(
        functools.partial(_mla_attn_kernel, tq=tq, tk=tk, nq=nq),
        grid_spec=grid_spec,
        out_shape=jax.ShapeDtypeStruct((batch * seq, MLA_W), BF16),
        compiler_params=pltpu.CompilerParams(dimension_semantics=("arbitrary",) * 2, vmem_limit_bytes=VMEM_LIMIT),
        name="mla_attn",
    )(*step_tabs, q, q, k, k, vt, gate)


def _diff_attn_kernel(qi_tab, kj_tab, first_tab, trip_diag, q_ref, k_ref, vt_ref, gate_ref, lam_ref, sg_ref, o_ref,
                      *scratch, tq, tk, nq, lam_init):
    slots, lim_ref, m_ref, acc_ref = _split_scratch(scratch, tk)
    n_strips = tq // MXU_DIM
    map_of_lane = _lane_iota((1, LANES)) // DIFF_D

    def scores(t, slot, masked):
        qi, kj = qi_tab[t], kj_tab[t]
        krows = pl.ds(pl.multiple_of(kj * tk, tk), tk)
        diag = (kj == qi).astype(jnp.int32) if masked else None

        def piece(idx, c):
            q_strip = q_ref[pl.ds(pl.multiple_of(qi * tq + c * MXU_DIM, MXU_DIM), MXU_DIM), :]
            q_map = jnp.where(map_of_lane == idx, q_strip, jnp.zeros_like(q_strip))
            _score_strip(k_ref[krows, :], q_map, lim_ref, diag, slots[slot], idx, c)

        return [functools.partial(piece, idx, c) for idx in range(4) for c in range(n_strips)]

    def update(t, slot):
        qi, kj = qi_tab[t], kj_tab[t]

        def piece(idx, c):
            hd = idx // 2
            _softmax_pv_strip(slots[slot], idx, c, vt_ref[kj, hd * VT_ROWS:(hd + 1) * VT_ROWS, :],
                              first_tab[t] != 0, m_ref, acc_ref, qi)

        return [functools.partial(piece, idx, c) for idx in range(4) for c in range(n_strips)]

    _flat_sweep(nq * (nq + 1) // 2, nq, trip_diag, scores, update)

    lv = lam_ref[...]
    lam = (jnp.exp(jnp.sum(lv[0:1] * lv[1:2], axis=-1, keepdims=True))
           - jnp.exp(jnp.sum(lv[2:3] * lv[3:4], axis=-1, keepdims=True)) + lam_init)

    def finish(qb, carry):
        rows = pl.ds(pl.multiple_of(qb * tq, tq), tq)
        heads_t = []
        for hd in range(2):
            o = _normalised_t(acc_ref, qb, 2 * hd) - lam * _normalised_t(acc_ref, qb, 2 * hd + 1)
            heads_t.append(o * lax.rsqrt(jnp.mean(o * o, axis=0, keepdims=True) + RMS_EPS))
        o = jnp.concatenate(heads_t, axis=0).T * sg_ref[...] * (1.0 - lam_init)
        o_ref[rows, :] = (o * gate_ref[rows, :].astype(F32)).astype(BF16)
        return carry

    lax.fori_loop(0, nq, finish, 0)


def _diff_attn_call(q, k, vt, gate, lam_pad, subln_t, layer, batch, seq, lam_init):
    tq, tk = TQ_DIFF, TK
    nq = seq // tq
    pairs = DIFF_HEADS // 2
    gate_off = MLA_W // LANES
    step_tabs = _causal_steps(nq)
    blk = lambda b, hp, *_: (b, hp)
    grid_spec = pltpu.PrefetchScalarGridSpec(
        num_scalar_prefetch=len(step_tabs),
        grid=(batch, pairs),
        in_specs=[pl.BlockSpec((seq, LANES), blk),
                  pl.BlockSpec((seq, LANES), blk),
                  pl.BlockSpec((seq // tk, 2 * VT_ROWS, tk), lambda b, hp, *_: (b, hp, 0)),
                  pl.BlockSpec((seq, LANES), lambda b, hp, *_: (b, gate_off + hp)),
                  _layer_spec(lam_pad, layer),
                  _layer_spec(subln_t, layer)],
        out_specs=pl.BlockSpec((seq, LANES), blk),
        scratch_shapes=_attn_scratch(4, nq, tk, tq))
    return pl.pallas_call(
        functools.partial(_diff_attn_kernel, tq=tq, tk=tk, nq=nq, lam_init=lam_init),
        grid_spec=grid_spec,
        out_shape=jax.ShapeDtypeStruct((batch * seq, DIFF_W), BF16),
        compiler_params=pltpu.CompilerParams(dimension_semantics=("arbitrary",) * 2, vmem_limit_bytes=VMEM_LIMIT),
        name="diff_attn",
    )(*step_tabs, q, k, vt, gate, lam_pad, subln_t)


def _out_kernel(x_ref, ymla_ref, ydiff_ref, ymem_ref, w_ref, o_ref):
    y = jnp.concatenate([ymla_ref[...], ydiff_ref[...], ymem_ref[...]], axis=1)
    o_ref[...] = x_ref[...] + jnp.dot(y, w_ref[...], preferred_element_type=F32)


def _out_call(x2d, y_mla, y_diff, y_mem, w_out, layer):
    n = x2d.shape[0]
    tm = TM_OUT
    row = lambda i: (i, 0)
    return pl.pallas_call(
        _out_kernel,
        grid=(n // tm,),
        in_specs=[pl.BlockSpec((tm, D_MODEL), row), pl.BlockSpec((tm, MLA_W), row),
                  pl.BlockSpec((tm, DIFF_W), row), pl.BlockSpec((tm, MEM_W), row),
                  _layer_spec(w_out, layer)],
        out_specs=pl.BlockSpec((tm, D_MODEL), row),
        out_shape=jax.ShapeDtypeStruct((n, D_MODEL), F32),
        compiler_params=pltpu.CompilerParams(dimension_semantics=("arbitrary",), vmem_limit_bytes=VMEM_LIMIT),
        name="out_proj",
    )(x2d, y_mla, y_diff, y_mem, w_out)


def _block_diag_ones(seg):
    idx = np.arange(MXU_DIM) // seg
    return jnp.asarray(idx[:, None] == idx[None, :], dtype=BF16)


def _pair_swapped(a):
    lead = a.shape[:-1]
    a4 = a.reshape(lead + (a.shape[-1] // ROPE_D, 2, ROPE_D // 2))
    return a4[..., ::-1, :].reshape(a.shape)


def _head_padded(a, heads, width, offset=0):
    lead = a.shape[:-1]
    a3 = a.reshape(lead + (heads, width))
    pad = [(0, 0)] * (a3.ndim - 1) + [(offset, LANES - width - offset)]
    return jnp.pad(a3, pad).reshape(lead + (heads * LANES,))


def _layer_spec(a, layer):
    zeros = (0,) * (a.ndim - 1)
    return pl.BlockSpec((None,) + a.shape[1:], lambda *_: (layer,) + zeros)


def _stacked_params(norm_g, w_in, mla_q_norm_g, mla_kv_norm_g, w_uq, w_ukv, mla_qn_g, mla_kn_g,
                    diff_qn_g, diff_kn_g, mem_qn_g):
    depth = w_in.shape[0]
    o_cq, o_ckv, o_kr = 0, Q_LORA, Q_LORA + KV_LORA
    o_dq = o_kr + MLA_ROPE
    o_dk = o_dq + DIFF_QK_W
    o_dv = o_dk + DIFF_QK_W
    o_mq = o_dv + DIFF_W
    o_z = o_mq + MEM_W
    w = w_in
    w_kr, w_dq, w_dk = w[..., o_kr:o_dq], w[..., o_dq:o_dk], w[..., o_dk:o_dv]
    w_in_p = jnp.concatenate([
        w[..., o_cq:o_ckv], w[..., o_ckv:o_kr],
        _head_padded(w_kr, 1, MLA_ROPE, MLA_NOPE), _head_padded(_pair_swapped(w_kr), 1, MLA_ROPE, MLA_NOPE),
        w_dq, _pair_swapped(w_dq), w_dk, _pair_swapped(w_dk),
        _head_padded(w[..., o_dv:o_mq], DIFF_HEADS, DIFF_V),
        w[..., o_mq:o_z], w[..., o_z:]], axis=-1).astype(BF16)

    uq4 = w_uq.reshape(depth, Q_LORA, MLA_HEADS, MLA_QK)
    w_uq_p = _head_padded(w_uq, MLA_HEADS, MLA_QK).astype(BF16)
    w_uq_sw = _head_padded(_pair_swapped(uq4[..., MLA_NOPE:]).reshape(depth, Q_LORA, MLA_HEADS * MLA_ROPE),
                           MLA_HEADS, MLA_ROPE, MLA_NOPE).astype(BF16)
    kv4 = w_ukv.reshape(depth, KV_LORA, MLA_HEADS, MLA_NOPE + MLA_V)
    w_uk_p = _head_padded(kv4[..., :MLA_NOPE].reshape(depth, KV_LORA, MLA_HEADS * MLA_NOPE),
                          MLA_HEADS, MLA_NOPE).astype(BF16)
    w_uv_t = jnp.pad(jnp.transpose(kv4[..., MLA_NOPE:], (0, 2, 3, 1)),
                     ((0, 0), (0, 0), (0, VT_ROWS - MLA_V), (0, 0))
                     ).reshape(depth, MLA_HEADS * VT_ROWS, KV_LORA).astype(BF16)

    q_scale = LOG2_E / math.sqrt(MLA_QK)
    d_scale = LOG2_E / math.sqrt(DIFF_D)
    row = lambda v: v[:, None, :]
    rope_vec = lambda g: row(jnp.pad(g[:, MLA_NOPE:], ((0, 0), (MLA_NOPE, LANES - MLA_QK))))
    rope_vec_sw = lambda g: row(jnp.pad(_pair_swapped(g[:, MLA_NOPE:]), ((0, 0), (MLA_NOPE, LANES - MLA_QK))))
    nope_vec = lambda g: row(jnp.pad(g[:, :MLA_NOPE], ((0, 0), (0, LANES - MLA_NOPE))))
    dqn = row(jnp.tile(diff_qn_g, (1, 2 * DIFF_HEADS)))
    dkn = row(jnp.tile(diff_kn_g, (1, 2 * DIFF_HEADS)))
    return {
        "gx": row(norm_g),
        "w_in": w_in_p,
        "gcq": row(mla_q_norm_g),
        "gckv": row(mla_kv_norm_g),
        "w_uq": w_uq_p, "w_uq_sw": w_uq_sw, "w_uk": w_uk_p, "w_uv_t": w_uv_t,
        "q_gc1": rope_vec(mla_qn_g) * q_scale, "q_gc0": nope_vec(mla_qn_g) * q_scale,
        "q_gs": rope_vec_sw(mla_qn_g) * q_scale,
        "k_gc1": rope_vec(mla_kn_g), "k_gc0": nope_vec(mla_kn_g), "k_gs": rope_vec_sw(mla_kn_g),
        "dq_gc": dqn * d_scale, "dq_gs": _pair_swapped(dqn) * d_scale,
        "dk_gc": dkn, "dk_gs": _pair_swapped(dkn),
        "mqn": row(jnp.tile(mem_qn_g, (1, MEM_HEADS))) * (1.0 / math.sqrt(MEM_D)),
        "bd_diff": jnp.broadcast_to(_block_diag_ones(DIFF_D), (depth, MXU_DIM, MXU_DIM)),
        "bd_mem": jnp.broadcast_to(_block_diag_ones(MEM_D), (depth, MXU_DIM, MXU_DIM)),
    }


def kernel(x, mem, positions, norm_g, w_in, mla_q_norm_g, mla_kv_norm_g, w_uq, w_ukv, mla_qn_g, mla_kn_g,
           diff_qn_g, diff_kn_g, diff_lambda, diff_subln_g, mem_norm_g, w_mem_kv, mem_qn_g, mem_kn_g, w_out):
    batch, seq, d_model = x.shape
    depth = w_in.shape[0]
    assert d_model == D_MODEL and MLA_V == DIFF_V and TQ_MLA == TK and TQ_DIFF == TK
    assert TRIP_STEPS % SCORE_SLOTS == 0 and SCORE_SLOTS >= 2 and TK % MXU_DIM == 0
    assert seq % max(TM_PROJ, TQ_MLA, TQ_DIFF, TM_OUT) == 0 and (batch * seq) % 1024 == 0
    cos_t, sin_t = _rope_tables(positions)
    kmem_all, vmem_all = _mem_kv(mem, mem_norm_g, w_mem_kv, mem_kn_g, _block_diag_ones(MEM_D))
    p = _stacked_params(norm_g, w_in, mla_q_norm_g, mla_kv_norm_g, w_uq, w_ukv, mla_qn_g, mla_kn_g,
                        diff_qn_g, diff_kn_g, mem_qn_g)
    lam_pad = jnp.pad(diff_lambda, ((0, 0), (0, 4), (0, LANES - DIFF_D)))
    subln_t = jnp.tile(diff_subln_g, (1, 2))[:, None, :]
    w_out_b = w_out.astype(BF16)
    x2d = x.reshape(batch * seq, D_MODEL)
    for l in range(depth):
        lam_init = 0.8 - 0.6 * math.exp(-0.3 * l)
        q, k, v, dq, dk, dv, y_mem, gate = _proj_call(x2d, cos_t, sin_t, kmem_all, vmem_all, p, l, seq)
        y_mla = _mla_attn_call(q, k, v, gate, batch, seq)
        y_diff = _diff_attn_call(dq, dk, dv, gate, lam_pad, subln_t, l, batch, seq, lam_init)
        x2d = _out_call(x2d, y_mla, y_diff, y_mem, w_out_b, l)
    return x2d.reshape(batch, seq, D_MODEL)
```
